```python
import jax, jax.numpy as jnp
from jax import lax
import numpy as np

D_MODEL = 1024
BATCH = 4
SEQ = 4096
DEPTH = 1

EPS = 1e-6
CONV_DIM = D_MODEL
CONV_WIDTH = 31
DN_HEADS = 8
DN_HEAD_DIM = 128
DN_DIM = DN_HEADS * DN_HEAD_DIM
SHORT_CONV = 4
CHUNK = 64
N_KEYS = 128
N_EXPERTS = N_KEYS * N_KEYS
PEER_HEADS = 8
PEER_KEY_DIM = 256
PEER_HALF = PEER_KEY_DIM // 2
PEER_TOPK = 16
PEER_BLOCK = 128
IN_DIM = 2 * CONV_DIM + 4 * DN_DIM + 2 * DN_HEADS + 2 * D_MODEL

kernel_name = "hybrid_conformer_gdn_peer_block"


def rms_norm(x, w):
    xf = x.astype(jnp.float32)
    y = xf * lax.rsqrt(jnp.mean(xf * xf, axis=-1, keepdims=True) + EPS)
    return (y * w.astype(jnp.float32)).astype(x.dtype)


def layer_norm(x, w, b):
    xf = x.astype(jnp.float32)
    mu = jnp.mean(xf, axis=-1, keepdims=True)
    xc = xf - mu
    y = xc * lax.rsqrt(jnp.mean(xc * xc, axis=-1, keepdims=True) + EPS)
    return (y * w.astype(jnp.float32) + b.astype(jnp.float32)).astype(x.dtype)


def l2_norm(x):
    xf = x.astype(jnp.float32)
    return xf * lax.rsqrt(jnp.sum(xf * xf, axis=-1, keepdims=True) + EPS)


def causal_depthwise_conv(x, w):
    width, c = w.shape
    return lax.conv_general_dilated(
        x, w[:, None, :].astype(x.dtype), window_strides=(1,),
        padding=[(width - 1, 0)], dimension_numbers=('NWC', 'WIO', 'NWC'),
        feature_group_count=c)


def gated_delta_rule(q, k, v, g, beta):
    b, s, h, dk = q.shape
    dv = v.shape[-1]
    n = s // CHUNK
    f32 = jnp.float32

    def to_chunks(t):
        t = jnp.moveaxis(t.astype(f32), 2, 1)
        return t.reshape((b, h, n, CHUNK) + t.shape[3:])

    q = to_chunks(q) * (dk ** -0.5)
    k, v, g, beta = to_chunks(k), to_chunks(v), to_chunks(g), to_chunks(beta)
    g = jnp.cumsum(g, axis=-1)
    idx = jnp.arange(CHUNK)
    causal = idx[:, None] >= idx[None, :]
    strict = idx[:, None] > idx[None, :]
    diff = g[..., :, None] - g[..., None, :]
    decay = jnp.where(causal, jnp.exp(jnp.where(causal, diff, 0.0)), 0.0)
    k_beta = k * beta[..., None]
    v_beta = v * beta[..., None]
    kk = jnp.einsum('bhnid,bhnjd->bhnij', k_beta, k) * decay
    a_mat = jnp.eye(CHUNK, dtype=f32) + jnp.where(strict, kk, 0.0)
    rhs = jnp.concatenate([v_beta, k_beta * jnp.exp(g)[..., None]], axis=-1)
    sol = lax.linalg.triangular_solve(a_mat, rhs, left_side=True, lower=True,
                                      unit_diagonal=True)
    u, w = sol[..., :dv], sol[..., dv:]
    qk = jnp.where(causal, jnp.einsum('bhnid,bhnjd->bhnij', q, k) * decay, 0.0)
    q_dec = q * jnp.exp(g)[..., None]
    k_dec = k * jnp.exp(g[..., -1:] - g)[..., None]
    g_last = jnp.exp(g[..., -1])

    def step(state, xs):
        qk_c, qd_c, kd_c, u_c, w_c, gl_c = xs
        v_new = u_c - jnp.einsum('bhcd,bhde->bhce', w_c, state)
        out = (jnp.einsum('bhcd,bhde->bhce', qd_c, state)
               + jnp.einsum('bhij,bhje->bhie', qk_c, v_new))
        state = state * gl_c[..., None, None] + jnp.einsum('bhcd,bhce->bhde', kd_c, v_new)
        return state, out

    xs = tuple(jnp.moveaxis(t, 2, 0) for t in (qk, q_dec, k_dec, u, w, g_last))
    state0 = jnp.zeros((b, h, dk, dv), f32)
    _, out = lax.scan(step, state0, xs)
    out = jnp.moveaxis(out, 0, 2).reshape(b, h, s, dv)
    return jnp.moveaxis(out, 1, 2)


def hybrid_mixer(xn, w_in, gate_bias, conv_dw_w, conv_dw_b, conv_ln_w, conv_ln_b,
                 w_conv_out, dn_conv_w, dn_a_log, dn_dt_bias, dn_norm_w, w_dn_out, w_out):
    b, s, _ = xn.shape
    proj = xn @ w_in
    offs = [CONV_DIM, 2 * CONV_DIM, 2 * CONV_DIM + 3 * DN_DIM, 2 * CONV_DIM + 4 * DN_DIM,
            2 * CONV_DIM + 4 * DN_DIM + DN_HEADS, 2 * CONV_DIM + 4 * DN_DIM + 2 * DN_HEADS,
            2 * CONV_DIM + 4 * DN_DIM + 2 * DN_HEADS + D_MODEL]
    conv_a, conv_b, qkv, z, a_dec, b_beta, gate_c, gate_d = jnp.split(proj, offs, axis=-1)

    hc = conv_a * jax.nn.sigmoid(conv_b)
    hc = causal_depthwise_conv(hc, conv_dw_w) + conv_dw_b
    hc = jax.nn.silu(layer_norm(hc, conv_ln_w, conv_ln_b))
    y_conv = hc @ w_conv_out

    qkv = jax.nn.silu(causal_depthwise_conv(qkv, dn_conv_w))
    q, k, v = jnp.split(qkv, 3, axis=-1)
    q = l2_norm(q.reshape(b, s, DN_HEADS, DN_HEAD_DIM))
    k = l2_norm(k.reshape(b, s, DN_HEADS, DN_HEAD_DIM))
    v = v.reshape(b, s, DN_HEADS, DN_HEAD_DIM)
    g = -jnp.exp(dn_a_log.astype(jnp.float32)) * jax.nn.softplus(
        a_dec.astype(jnp.float32) + dn_dt_bias.astype(jnp.float32))
    beta = jax.nn.sigmoid(b_beta.astype(jnp.float32))
    o = gated_delta_rule(q, k, v, g, beta)
    zf = z.reshape(b, s, DN_HEADS, DN_HEAD_DIM).astype(jnp.float32)
    o = rms_norm(o, dn_norm_w) * jax.nn.silu(zf)
    y_dn = o.reshape(b, s, DN_DIM).astype(xn.dtype) @ w_dn_out

    merged = (jax.nn.sigmoid(gate_c + gate_bias[0]) * y_conv
              + jax.nn.sigmoid(gate_d + gate_bias[1]) * y_dn)
    return merged @ w_out


def peer_ffn(x, w_query, sub_keys, expert_down, expert_up):
    b, s, d = x.shape
    t = b * s
    xt = x.reshape(t, d)
    q = (xt @ w_query).reshape(t, PEER_HEADS, 2, PEER_HALF).astype(jnp.float32)
    scores = jnp.einsum('thpc,hpnc->thpn', q, sub_keys.astype(jnp.float32))
    s_top, i_top = lax.top_k(scores, PEER_TOPK)
    cand = s_top[:, :, 0, :, None] + s_top[:, :, 1, None, :]
    cand_idx = i_top[:, :, 0, :, None] * N_KEYS + i_top[:, :, 1, None, :]
    best, pos = lax.top_k(cand.reshape(t, PEER_HEADS, PEER_TOPK * PEER_TOPK), PEER_TOPK)
    experts = jnp.take_along_axis(
        cand_idx.reshape(t, PEER_HEADS, PEER_TOPK * PEER_TOPK), pos, axis=-1)
    gates = jax.nn.softmax(best, axis=-1)

    def block(args):
        xb, eb, gb = args
        u = jnp.take(expert_down, eb, axis=0)
        act = jax.nn.gelu(jnp.einsum('thkd,td->thk', u, xb), approximate=False)
        coef = (gb * act.astype(jnp.float32)).astype(xb.dtype)
        vv = jnp.take(expert_up, eb, axis=0)
        return jnp.einsum('thk,thkd->td', coef, vv)

    n_blk = t // PEER_BLOCK
    out = lax.map(block, (xt.reshape(n_blk, PEER_BLOCK, d),
                          experts.reshape(n_blk, PEER_BLOCK, PEER_HEADS, PEER_TOPK),
                          gates.reshape(n_blk, PEER_BLOCK, PEER_HEADS, PEER_TOPK)))
    return out.reshape(b, s, d)


def setup_inputs(seed: int = 0) -> dict:
    key = jax.random.key(seed)
    ks = jax.random.split(key, 24)
    f32 = jnp.float32
    nrm = lambda k, shape, scale: jax.random.normal(k, shape, f32) * scale
    dt = jax.random.uniform(ks[12], (DEPTH, DN_HEADS), f32, 0.001, 0.1)
    return {
        "x": nrm(ks[0], (BATCH, SEQ, D_MODEL), 1.0),
        "norm_mix_w": 1.0 + nrm(ks[1], (DEPTH, D_MODEL), 0.01),
        "w_in": nrm(ks[2], (DEPTH, D_MODEL, IN_DIM), D_MODEL ** -0.5),
        "gate_bias": nrm(ks[3], (DEPTH, 2, D_MODEL), 0.01),
        "conv_dw_w": nrm(ks[4], (DEPTH, CONV_WIDTH, CONV_DIM), CONV_WIDTH ** -0.5),
        "conv_dw_b": nrm(ks[5], (DEPTH, CONV_DIM), 0.01),
        "conv_ln_w": 1.0 + nrm(ks[6], (DEPTH, CONV_DIM), 0.01),
        "conv_ln_b": nrm(ks[7], (DEPTH, CONV_DIM), 0.01),
        "w_conv_out": nrm(ks[8], (DEPTH, CONV_DIM, D_MODEL), CONV_DIM ** -0.5),
        "dn_conv_w": nrm(ks[9], (DEPTH, SHORT_CONV, 3 * DN_DIM), SHORT_CONV ** -0.5),
        "dn_a_log": jnp.log(jax.random.uniform(ks[10], (DEPTH, DN_HEADS), f32, 1.0, 16.0)),
        "dn_dt_bias": jnp.log(jnp.expm1(dt)) + nrm(ks[11], (DEPTH, DN_HEADS), 0.01),
        "dn_norm_w": 1.0 + nrm(ks[13], (DEPTH, DN_HEAD_DIM), 0.01),
        "w_dn_out": nrm(ks[14], (DEPTH, DN_DIM, D_MODEL), DN_DIM ** -0.5),
        "w_out": nrm(ks[15], (DEPTH, D_MODEL, D_MODEL), D_MODEL ** -0.5),
        "norm_ffn_w": 1.0 + nrm(ks[16], (DEPTH, D_MODEL), 0.01),
        "peer_w_query": nrm(ks[17], (DEPTH, D_MODEL, PEER_HEADS * PEER_KEY_DIM), D_MODEL ** -0.5),
        "peer_sub_keys": nrm(ks[18], (DEPTH, PEER_HEADS, 2, N_KEYS, PEER_HALF), PEER_HALF ** -0.5),
        "peer_down": nrm(ks[19], (DEPTH, N_EXPERTS, D_MODEL), D_MODEL ** -0.5),
        "peer_up": nrm(ks[20], (DEPTH, N_EXPERTS, D_MODEL), 0.5),
        "final_norm_w": 1.0 + nrm(ks[21], (D_MODEL,), 0.01),
    }


def reference(x, norm_mix_w, w_in, gate_bias, conv_dw_w, conv_dw_b, conv_ln_w, conv_ln_b,
              w_conv_out, dn_conv_w, dn_a_log, dn_dt_bias, dn_norm_w, w_dn_out, w_out,
              norm_ffn_w, peer_w_query, peer_sub_keys, peer_down, peer_up, final_norm_w):
    h = x
    for l in range(DEPTH):
        xn = rms_norm(h, norm_mix_w[l])
        h = h + hybrid_mixer(xn, w_in[l], gate_bias[l], conv_dw_w[l], conv_dw_b[l],
                             conv_ln_w[l], conv_ln_b[l], w_conv_out[l], dn_conv_w[l],
                             dn_a_log[l], dn_dt_bias[l], dn_norm_w[l], w_dn_out[l], w_out[l])
        xn = rms_norm(h, norm_ffn_w[l])
        h = h + peer_ffn(xn, peer_w_query[l], peer_sub_keys[l], peer_down[l], peer_up[l])
    return rms_norm(h, final_norm_w)
```

```python
import functools

import jax
import jax.numpy as jnp
from jax import lax
from jax.experimental import pallas as pl
from jax.experimental.pallas import tpu as pltpu

F32 = jnp.float32
BF16 = jnp.bfloat16
EPS = 1e-6
HIGHEST = lax.Precision.HIGHEST

CONV_WIDTH = 31
SHORT_CONV = 4
DN_HEADS = 8
DN_HEAD_DIM = 128
CHUNK = 64
N_KEYS = 128
PEER_HEADS = 8
PEER_HALF = 128
PEER_TOPK = 16

LANES = 128
SUBLANES = 8
VMEM_LIMIT = 56 * 1024 * 1024


def _params(semantics):
    return pltpu.CompilerParams(dimension_semantics=semantics, vmem_limit_bytes=VMEM_LIMIT)


def _dot(a, b, precision=None):
    return jnp.dot(a, b, preferred_element_type=F32, precision=precision)


def _dot_nt(a, b, precision=None):
    return lax.dot_general(a, b, (((1,), (1,)), ((), ())), preferred_element_type=F32,
                           precision=precision)


def _sigmoid(x):
    return jax.nn.sigmoid(x)


def _silu(x):
    return x * jax.nn.sigmoid(x)


def _gelu(x):
    return 0.5 * x * (1.0 + lax.erf(x * (2.0 ** -0.5)))


def _inproj_kernel(x_ref, nw_ref, w_ref, o_ref, xn_ref):
    @pl.when(pl.program_id(1) == 0)
    def _():
        x = x_ref[...]
        y = x * lax.rsqrt(jnp.mean(x * x, axis=-1, keepdims=True) + EPS)
        xn_ref[...] = (y * nw_ref[...]).astype(BF16)

    o_ref[...] = _dot(xn_ref[...], w_ref[...])


def _inproj(x2d, norm_w, w_bf16, tm=1024, tn=512):
    t, d = x2d.shape
    n = w_bf16.shape[1]
    return pl.pallas_call(
        _inproj_kernel,
        grid=(t // tm, n // tn),
        in_specs=[pl.BlockSpec((tm, d), lambda i, j: (i, 0)),
                  pl.BlockSpec((1, d), lambda i, j: (0, 0)),
                  pl.BlockSpec((d, tn), lambda i, j: (0, j))],
        out_specs=pl.BlockSpec((tm, tn), lambda i, j: (i, j)),
        out_shape=jax.ShapeDtypeStruct((t, n), F32),
        scratch_shapes=[pltpu.VMEM((tm, d), BF16)],
        compiler_params=_params(("parallel", "arbitrary")),
    )(x2d, norm_w, w_bf16)


CONV_HALO = 32
CONV_ROWS = 128


def _conv_kernel(a_ref, b_ref, ha_ref, hb_ref, gc_ref, dww_ref, dwb_ref, lnw_ref, lnb_ref,
                 wco_ref, gb_ref, o_ref, buf_ref, cv_ref, hb16_ref):
    ts, c = a_ref.shape
    first = pl.program_id(1) == 0
    halo = ha_ref[...] * _sigmoid(hb_ref[...])
    buf_ref[0:CONV_HALO, :] = jnp.where(first, 0.0, halo)
    buf_ref[CONV_HALO:, :] = a_ref[...] * _sigmoid(b_ref[...])
    shift = CONV_HALO - (CONV_WIDTH - 1)

    def strip(s, carry):
        lane = pl.ds(pl.multiple_of(s * LANES, LANES), LANES)
        for rb in range(ts // CONV_ROWS):
            r0 = rb * CONV_ROWS + shift
            acc = jnp.zeros((CONV_ROWS, LANES), F32)
            for j in range(CONV_WIDTH):
                acc = acc + dww_ref[j:j + 1, lane] * buf_ref[r0 + j:r0 + j + CONV_ROWS, lane]
            cv_ref[rb * CONV_ROWS:(rb + 1) * CONV_ROWS, lane] = acc + dwb_ref[:, lane]
        return carry

    lax.fori_loop(0, c // LANES, strip, 0)

    def ln_rows(r, carry):
        rows = pl.ds(pl.multiple_of(r * 64, 64), 64)
        v = cv_ref[rows, :]
        mu = jnp.mean(v, axis=-1, keepdims=True)
        vc = v - mu
        y = vc * lax.rsqrt(jnp.mean(vc * vc, axis=-1, keepdims=True) + EPS)
        y = y * lnw_ref[...] + lnb_ref[...]
        hb16_ref[rows, :] = _silu(y).astype(BF16)
        return carry

    lax.fori_loop(0, ts // 64, ln_rows, 0)
    y_conv = _dot(hb16_ref[...], wco_ref[...])
    o_ref[...] = _sigmoid(gc_ref[...] + gb_ref[...]) * y_conv


def _conv_branch(proj, batch, seq, conv_dw_w, conv_dw_b, conv_ln_w, conv_ln_b, w_conv_out_bf16,
                 gate_bias0, ts=512):
    t = proj.shape[0]
    c = conv_dw_w.shape[1]
    nt = seq // ts
    hpt = ts // CONV_HALO

    def cur(col):
        return pl.BlockSpec((ts, c), lambda b, i: (b * nt + i, col))

    def halo(col):
        return pl.BlockSpec((CONV_HALO, c),
                            lambda b, i: (jnp.maximum((b * nt + i) * hpt - 1, 0), col))

    def row(n):
        return pl.BlockSpec((n, c), lambda b, i: (0, 0))

    return pl.pallas_call(
        _conv_kernel,
        grid=(batch, nt),
        in_specs=[cur(0), cur(1), halo(0), halo(1), cur(6),
                  row(CONV_WIDTH), row(1), row(1), row(1),
                  pl.BlockSpec((c, c), lambda b, i: (0, 0)), row(1)],
        out_specs=pl.BlockSpec((ts, c), lambda b, i: (b * nt + i, 0)),
        out_shape=jax.ShapeDtypeStruct((t, c), F32),
        scratch_shapes=[pltpu.VMEM((ts + CONV_HALO, c), F32), pltpu.VMEM((ts, c), F32),
                        pltpu.VMEM((ts, c), BF16)],
        compiler_params=_params(("parallel", "arbitrary")),
    )(proj, proj, proj, proj, proj, conv_dw_w, conv_dw_b, conv_ln_w, conv_ln_b,
      w_conv_out_bf16, gate_bias0)


DN_HEADS_PER_STEP = 2
DN_CARRY = 8
DN_PRE_ROWS = 128


def _tri_inverse(l_mat, eye, blk16, blk32, precision):
    mm = functools.partial(_dot, precision=precision)
    ld = jnp.where(blk16, l_mat, 0.0)
    p2 = mm(ld, ld)
    p4 = mm(p2, p2)
    p8 = mm(p4, p4)
    td = mm(mm(mm(eye - ld, eye + p2), eye + p4), eye + p8)
    c32 = jnp.where(jnp.logical_and(blk32, jnp.logical_not(blk16)), l_mat, 0.0)
    t32 = td - mm(td, mm(c32, td))
    c64 = jnp.where(blk32, 0.0, l_mat)
    return t32 - mm(t32, mm(c64, t32))


def _delta_kernel(q_ref, k_ref, v_ref, z_ref, a_ref, b_ref, cwq_ref, cwk_ref, cwv_ref,
                  alog_ref, dtb_ref, nw_ref, o_ref,
                  qraw_ref, kraw_ref, vraw_ref, qs_ref, ks_ref, vs_ref, state_ref):
    ts = q_ref.shape[0]
    nh = q_ref.shape[1] // DN_HEAD_DIM
    prec = HIGHEST
    mm = functools.partial(_dot, precision=prec)
    mm_nt = functools.partial(_dot_nt, precision=prec)

    @pl.when(pl.program_id(2) == 0)
    def _():
        zeros = jnp.zeros((DN_CARRY, q_ref.shape[1]), F32)
        qraw_ref[0:DN_CARRY, :] = zeros
        kraw_ref[0:DN_CARRY, :] = zeros
        vraw_ref[0:DN_CARRY, :] = zeros
        state_ref[...] = jnp.zeros_like(state_ref)

    for raw_ref, src_ref, cw_ref, dst_ref, norm in (
            (qraw_ref, q_ref, cwq_ref, qs_ref, True),
            (kraw_ref, k_ref, cwk_ref, ks_ref, True),
            (vraw_ref, v_ref, cwv_ref, vs_ref, False)):
        raw_ref[DN_CARRY:, :] = src_ref[...]
        for rb in range(ts // DN_PRE_ROWS):
            r0 = rb * DN_PRE_ROWS + DN_CARRY - (SHORT_CONV - 1)
            acc = jnp.zeros((DN_PRE_ROWS, raw_ref.shape[1]), F32)
            for j in range(SHORT_CONV):
                acc = acc + cw_ref[j:j + 1, :] * raw_ref[r0 + j:r0 + j + DN_PRE_ROWS, :]
            acc = _silu(acc)
            rows = slice(rb * DN_PRE_ROWS, (rb + 1) * DN_PRE_ROWS)
            if norm:
                for h in range(nh):
                    cols = slice(h * DN_HEAD_DIM, (h + 1) * DN_HEAD_DIM)
                    ah = acc[:, cols]
                    dst_ref[rows, cols] = ah * lax.rsqrt(
                        jnp.sum(ah * ah, axis=-1, keepdims=True) + EPS)
            else:
                dst_ref[rows, :] = acc
        raw_ref[0:DN_CARRY, :] = raw_ref[ts:ts + DN_CARRY, :]

    ri = lax.broadcasted_iota(jnp.int32, (CHUNK, CHUNK), 0)
    ci = lax.broadcasted_iota(jnp.int32, (CHUNK, CHUNK), 1)
    causal = ri >= ci
    strict = ri > ci
    eye = jnp.where(ri == ci, 1.0, 0.0).astype(F32)
    tri_incl = jnp.where(causal, 1.0, 0.0).astype(F32)
    blk16 = (ri // 16) == (ci // 16)
    blk32 = (ri // 32) == (ci // 32)
    scale = DN_HEAD_DIM ** -0.5

    def chunk_body(cidx, carry):
        rows = pl.ds(pl.multiple_of(cidx * CHUNK, CHUNK), CHUNK)
        for h in range(nh):
            cols = slice(h * DN_HEAD_DIM, (h + 1) * DN_HEAD_DIM)
            qc = qs_ref[rows, cols] * scale
            kc = ks_ref[rows, cols]
            vc = vs_ref[rows, cols]
            g = -jnp.exp(alog_ref[:, cols]) * jax.nn.softplus(a_ref[rows, cols] + dtb_ref[:, cols])
            beta = _sigmoid(b_ref[rows, cols])
            gcum = mm(tri_incl, g)
            g_row = gcum.T[0:CHUNK, :]
            g_col = gcum[:, 0:CHUNK]
            decay = jnp.where(causal, jnp.exp(jnp.where(causal, g_col - g_row, 0.0)), 0.0)
            eg = jnp.exp(gcum)
            kb = kc * beta
            vb = vc * beta
            l_mat = jnp.where(strict, mm_nt(kb, kc) * decay, 0.0)
            t_inv = _tri_inverse(l_mat, eye, blk16, blk32, prec)
            u = mm(t_inv, vb)
            w = mm(t_inv, kb * eg)
            qk = jnp.where(causal, mm_nt(qc, kc) * decay, 0.0)
            g_last = gcum[CHUNK - 1:CHUNK, :]
            q_dec = qc * eg
            k_dec = kc * jnp.exp(g_last - gcum)
            state = state_ref[h]
            v_new = u - mm(w, state)
            out = mm(q_dec, state) + mm(qk, v_new)
            state_ref[h] = state * jnp.exp(g_last) + mm(k_dec.T, v_new)
            y = out * lax.rsqrt(jnp.mean(out * out, axis=-1, keepdims=True) + EPS) * nw_ref[...]
            o_ref[rows, cols] = (y * _silu(z_ref[rows, cols])).astype(o_ref.dtype)
        return carry

    lax.fori_loop(0, ts // CHUNK, chunk_body, 0)


def _delta_branch(proj, batch, seq, dn_conv_w, alog_rep, dtb_rep, dn_norm_w, ts=512):
    t = proj.shape[0]
    hb = DN_HEADS_PER_STEP
    wcols = hb * DN_HEAD_DIM
    dn_dim = DN_HEADS * DN_HEAD_DIM
    nt = seq // ts
    per_group = dn_dim // wcols

    def col(group):
        return pl.BlockSpec((ts, wcols), lambda b, h, s: (b * nt + s, group * per_group + h))

    def cw(group):
        return pl.BlockSpec((SHORT_CONV, wcols), lambda b, h, s: (0, group * per_group + h))

    head_row = pl.BlockSpec((1, wcols), lambda b, h, s: (0, h))
    return pl.pallas_call(
        _delta_kernel,
        grid=(batch, DN_HEADS // hb, nt),
        in_specs=[col(2), col(3), col(4), col(5), col(8), col(9), cw(0), cw(1), cw(2),
                  head_row, head_row, pl.BlockSpec((1, DN_HEAD_DIM), lambda b, h, s: (0, 0))],
        out_specs=pl.BlockSpec((ts, wcols), lambda b, h, s: (b * nt + s, h)),
        out_shape=jax.ShapeDtypeStruct((t, dn_dim), BF16),
        scratch_shapes=[pltpu.VMEM((ts + DN_CARRY, wcols), F32)] * 3
                       + [pltpu.VMEM((ts, wcols), F32)] * 3
                       + [pltpu.VMEM((hb, DN_HEAD_DIM, DN_HEAD_DIM), F32)],
        compiler_params=_params(("parallel", "parallel", "arbitrary")),
    )(proj, proj, proj, proj, proj, proj, dn_conv_w, dn_conv_w, dn_conv_w,
      alog_rep, dtb_rep, dn_norm_w)


def _merge_kernel(x_ref, mc_ref, o_ref, gd_ref, gb_ref, wdn_ref, wout_ref, nw_ref, wq_ref,
                  h_ref, xn_ref, q_ref):
    y_dn = _dot(o_ref[...], wdn_ref[...])
    merged = mc_ref[...] + _sigmoid(gd_ref[...] + gb_ref[...]) * y_dn
    h = x_ref[...] + _dot(merged.astype(BF16), wout_ref[...])
    h_ref[...] = h
    y = h * lax.rsqrt(jnp.mean(h * h, axis=-1, keepdims=True) + EPS) * nw_ref[...]
    xn = y.astype(BF16)
    xn_ref[...] = xn
    q_ref[...] = _dot(xn, wq_ref[...])


def _merge(x2d, mc, o_gated, proj, gate_bias1, w_dn_out_bf16, w_out_bf16, norm_ffn_w,
           w_query_bf16, tm=512):
    t, d = x2d.shape
    nq = w_query_bf16.shape[1]
    tile = pl.BlockSpec((tm, d), lambda i: (i, 0))
    row = pl.BlockSpec((1, d), lambda i: (0, 0))
    full = pl.BlockSpec((d, d), lambda i: (0, 0))
    return pl.pallas_call(
        _merge_kernel,
        grid=(t // tm,),
        in_specs=[tile, tile, tile, pl.BlockSpec((tm, d), lambda i: (i, 7)), row, full, full,
                  row, pl.BlockSpec((d, nq), lambda i: (0, 0))],
        out_specs=[tile, tile, pl.BlockSpec((tm, nq), lambda i: (i, 0))],
        out_shape=[jax.ShapeDtypeStruct((t, d), F32), jax.ShapeDtypeStruct((t, d), BF16),
                   jax.ShapeDtypeStruct((t, nq), F32)],
        compiler_params=_params(("parallel",)),
    )(x2d, mc, o_gated, proj, gate_bias1, w_dn_out_bf16, w_out_bf16, norm_ffn_w, w_query_bf16)


def _top16(scores, want_tops):
    n = scores.shape[0]
    iota = lax.broadcasted_iota(jnp.int32, scores.shape, 0)
    rank = jnp.full(scores.shape, float(PEER_TOPK), F32)
    tops = []
    work = scores
    for r in range(PEER_TOPK):
        m = jnp.max(work, axis=0, keepdims=True)
        idx = jnp.min(jnp.where(work == m, iota, n), axis=0, keepdims=True)
        hit = iota == idx
        rank = jnp.where(hit, float(r), rank)
        work = jnp.where(hit, -jnp.inf, work)
        if want_tops:
            tops.append(m)
    return rank, (jnp.concatenate(tops, axis=0) if want_tops else None)


def _route_kernel(q_ref, keys_ref, e1_ref, cnt1_ref, e2_ref, rank2_ref):
    def head(h, carry):
        def scores(p):
            cols = pl.ds(pl.multiple_of((2 * h + p) * PEER_HALF, PEER_HALF), PEER_HALF)
            return _dot_nt(keys_ref[2 * h + p], q_ref[:, cols], precision=HIGHEST)

        s1 = scores(0)
        s2 = scores(1)
        rank1, top1 = _top16(s1, True)
        rank2, top2 = _top16(s2, True)
        cand = jnp.concatenate([top1[k:k + 1, :] + top2 for k in range(PEER_TOPK)], axis=0)
        crank, _ = _top16(cand, False)
        sel = crank < float(PEER_TOPK)
        ex = jnp.where(sel, jnp.exp(cand - cand[0:1, :]), 0.0)
        z = jnp.sum(ex, axis=0, keepdims=True)
        cnt1 = jnp.zeros(s1.shape, F32)
        for k in range(PEER_TOPK):
            blk = jnp.where(sel[k * PEER_TOPK:(k + 1) * PEER_TOPK, :], 1.0, 0.0)
            ck = jnp.sum(blk, axis=0, keepdims=True)
            cnt1 = jnp.where(rank1 == float(k), ck, cnt1)
        e1_ref[h] = jnp.exp(s1 - top1[0:1, :])
        cnt1_ref[h] = cnt1
        e2_ref[h] = jnp.exp(s2 - top2[0:1, :]) / z
        rank2_ref[h] = rank2
        return carry

    lax.fori_loop(0, PEER_HEADS, head, 0)


def _route(q, keys, tt=128):
    t = q.shape[0]
    out = jax.ShapeDtypeStruct((PEER_HEADS, N_KEYS, t), F32)
    spec = pl.BlockSpec((PEER_HEADS, N_KEYS, tt), lambda i: (0, 0, i))
    return pl.pallas_call(
        _route_kernel,
        grid=(t // tt,),
        in_specs=[pl.BlockSpec((tt, q.shape[1]), lambda i: (i, 0)),
                  pl.BlockSpec(keys.shape, lambda i: (0, 0, 0))],
        out_specs=[spec] * 4,
        out_shape=[out] * 4,
        compiler_params=_params(("parallel",)),
    )(q, keys)


PEER_ROWS_PER_STEP = SUBLANES


def _peer_kernel(xn_ref, down_ref, upt_ref, e1_ref, cnt1_ref, e2_ref, rank2_ref, h_ref, nw_ref,
                 o_ref, acc_ref, act_ref, ct_ref):
    j = pl.program_id(1)
    tt = xn_ref.shape[0]

    @pl.when(j == 0)
    def _():
        acc_ref[...] = jnp.zeros_like(acc_ref)

    act_ref[...] = _dot_nt(down_ref[...], xn_ref[...])
    group = pl.ds(pl.multiple_of(j * PEER_ROWS_PER_STEP, SUBLANES), PEER_ROWS_PER_STEP)

    def lane_tile(lt, carry):
        lanes = pl.ds(pl.multiple_of(lt * LANES, LANES), LANES)
        for ib in range(PEER_ROWS_PER_STEP):
            rows = slice(ib * N_KEYS, (ib + 1) * N_KEYS)
            wsum = jnp.zeros((N_KEYS, LANES), F32)
            for h in range(PEER_HEADS):
                p = e1_ref[h, group, lanes][ib:ib + 1, :]
                c = cnt1_ref[h, group, lanes][ib:ib + 1, :]
                wsum = wsum + jnp.where(rank2_ref[h, :, lanes] < c, e2_ref[h, :, lanes], 0.0) * p
            ct_ref[rows, lanes] = (wsum * _gelu(act_ref[rows, lanes])).astype(BF16)
        return carry

    lax.fori_loop(0, tt // LANES, lane_tile, 0)
    acc_ref[...] += _dot(upt_ref[...], ct_ref[...])

    @pl.when(j == pl.num_programs(1) - 1)
    def _():
        h = h_ref[...] + acc_ref[...].T
        o_ref[...] = h * lax.rsqrt(jnp.mean(h * h, axis=-1, keepdims=True) + EPS) * nw_ref[...]


def _peer(xn2, down_bf16, upt_bf16, e1, cnt1, e2, rank2, h1, final_norm_w, tt=512):
    t, d = xn2.shape
    eb = PEER_ROWS_PER_STEP * N_KEYS
    n_exp = down_bf16.shape[0]
    fac = pl.BlockSpec((PEER_HEADS, N_KEYS, tt), lambda i, j: (0, 0, i))
    tile = pl.BlockSpec((tt, d), lambda i, j: (i, 0))
    return pl.pallas_call(
        _peer_kernel,
        grid=(t // tt, n_exp // eb),
        in_specs=[tile, pl.BlockSpec((eb, d), lambda i, j: (j, 0)),
                  pl.BlockSpec((d, eb), lambda i, j: (0, j)), fac, fac, fac, fac, tile,
                  pl.BlockSpec((1, d), lambda i, j: (0, 0))],
        out_specs=tile,
        out_shape=jax.ShapeDtypeStruct((t, d), F32),
        scratch_shapes=[pltpu.VMEM((d, tt), F32), pltpu.VMEM((eb, tt), F32),
                        pltpu.VMEM((eb, tt), BF16)],
        compiler_params=_params(("parallel", "arbitrary")),
    )(xn2, down_bf16, upt_bf16, e1, cnt1, e2, rank2, h1, final_norm_w)


def _layer(h2d, batch, seq, norm_mix_w, w_in, gate_bias, conv_dw_w, conv_dw_b, conv_ln_w,
           conv_ln_b, w_conv_out, dn_conv_w, dn_a_log, dn_dt_bias, dn_norm_w, w_dn_out, w_out,
           norm_ffn_w, peer_w_query, peer_sub_keys, peer_down, peer_up, out_norm_w):
    d = h2d.shape[1]
    conv_dim = conv_dw_w.shape[1]
    dn_dim = DN_HEADS * DN_HEAD_DIM
    o_qkv = 2 * conv_dim
    o_z = o_qkv + 3 * dn_dim
    o_a = o_z + dn_dim
    o_b = o_a + DN_HEADS
    o_gc = o_b + DN_HEADS
    w1 = jnp.concatenate([w_in[:, :o_a], w_in[:, o_gc:],
                          jnp.repeat(w_in[:, o_a:o_b], DN_HEAD_DIM, axis=1),
                          jnp.repeat(w_in[:, o_b:o_gc], DN_HEAD_DIM, axis=1)], axis=1).astype(BF16)
    proj = _inproj(h2d, norm_mix_w.reshape(1, d), w1)
    mc = _conv_branch(proj, batch, seq, conv_dw_w, conv_dw_b.reshape(1, -1),
                      conv_ln_w.reshape(1, -1), conv_ln_b.reshape(1, -1),
                      w_conv_out.astype(BF16), gate_bias[0].reshape(1, d))
    o_gated = _delta_branch(proj, batch, seq, dn_conv_w,
                            jnp.repeat(dn_a_log, DN_HEAD_DIM).reshape(1, dn_dim),
                            jnp.repeat(dn_dt_bias, DN_HEAD_DIM).reshape(1, dn_dim),
                            dn_norm_w.reshape(1, DN_HEAD_DIM))
    h1, xn2, q = _merge(h2d, mc, o_gated, proj, gate_bias[1].reshape(1, d),
                        w_dn_out.astype(BF16), w_out.astype(BF16), norm_ffn_w.reshape(1, d),
                        peer_w_query.astype(BF16))
    keys = peer_sub_keys.reshape(PEER_HEADS * 2, N_KEYS, PEER_HALF)
    e1, cnt1, e2, rank2 = _route(q, keys)
    return _peer(xn2, peer_down.astype(BF16), peer_up.T.astype(BF16), e1, cnt1, e2, rank2, h1,
                 out_norm_w.reshape(1, d))


def kernel(x, norm_mix_w, w_in, gate_bias, conv_dw_w, conv_dw_b, conv_ln_w, conv_ln_b,
           w_conv_out, dn_conv_w, dn_a_log, dn_dt_bias, dn_norm_w, w_dn_out, w_out, norm_ffn_w,
           peer_w_query, peer_sub_keys, peer_down, peer_up, final_norm_w):
    batch, seq, d = x.shape
    depth = w_in.shape[0]
    assert depth == 1, "the final RMSNorm is fused into the (single) layer's last stage"
    out = _layer(x.reshape(batch * seq, d), batch, seq, norm_mix_w[0], w_in[0], gate_bias[0],
                 conv_dw_w[0], conv_dw_b[0], conv_ln_w[0], conv_ln_b[0], w_conv_out[0],
                 dn_conv_w[0], dn_a_log[0], dn_dt_bias[0], dn_norm_w[0], w_dn_out[0], w_out[0],
                 norm_ffn_w[0], peer_w_query[0], peer_sub_keys[0], peer_down[0], peer_up[0],
                 final_norm_w)
    return out.reshape(batch, seq, d)
```

```python
import functools

import jax
import jax.numpy as jnp
from jax import lax
from jax.experimental import pallas as pl
from jax.experimental.pallas import tpu as pltpu

F32 = jnp.float32
BF16 = jnp.bfloat16
EPS = 1e-6
HIGHEST = lax.Precision.HIGHEST

CONV_WIDTH = 31
SHORT_CONV = 4
DN_HEADS = 8
DN_HEAD_DIM = 128
CHUNK = 64
N_KEYS = 128
PEER_HEADS = 8
PEER_HALF = 128
PEER_TOPK = 16

LANES = 128
SUBLANES = 8
VMEM_LIMIT = 56 * 1024 * 1024


def _params(semantics):
    return pltpu.CompilerParams(dimension_semantics=semantics, vmem_limit_bytes=VMEM_LIMIT)


def _dot(a, b, precision=None):
    return jnp.dot(a, b, preferred_element_type=F32, precision=precision)


def _dot_nt(a, b, precision=None):
    return lax.dot_general(a, b, (((1,), (1,)), ((), ())), preferred_element_type=F32,
                           precision=precision)


def _sigmoid(x):
    return jax.nn.sigmoid(x)


def _silu(x):
    return x * jax.nn.sigmoid(x)


def _gelu(x):
    return 0.5 * x * (1.0 + lax.erf(x * (2.0 ** -0.5)))


def _inproj_kernel(x_ref, nw_ref, w_ref, o_ref, xn_ref):
    @pl.when(pl.program_id(1) == 0)
    def _():
        x = x_ref[...]
        y = x * lax.rsqrt(jnp.mean(x * x, axis=-1, keepdims=True) + EPS)
        xn_ref[...] = (y * nw_ref[...]).astype(BF16)

    o_ref[...] = _dot(xn_ref[...], w_ref[...])


def _inproj(x2d, norm_w, w_bf16, tm=1024, tn=512):
    t, d = x2d.shape
    n = w_bf16.shape[1]
    return pl.pallas_call(
        _inproj_kernel,
        grid=(t // tm, n // tn),
        in_specs=[pl.BlockSpec((tm, d), lambda i, j: (i, 0)),
                  pl.BlockSpec((1, d), lambda i, j: (0, 0)),
                  pl.BlockSpec((d, tn), lambda i, j: (0, j))],
        out_specs=pl.BlockSpec((tm, tn), lambda i, j: (i, j)),
        out_shape=jax.ShapeDtypeStruct((t, n), F32),
        scratch_shapes=[pltpu.VMEM((tm, d), BF16)],
        compiler_params=_params(("parallel", "arbitrary")),
    )(x2d, norm_w, w_bf16)


CONV_HALO = 32
CONV_ROWS = 128


def _conv_kernel(a_ref, b_ref, ha_ref, hb_ref, gc_ref, dww_ref, dwb_ref, lnw_ref, lnb_ref,
                 wco_ref, gb_ref, o_ref, buf_ref, cv_ref, hb16_ref):
    ts, c = a_ref.shape
    first = pl.program_id(1) == 0
    halo = ha_ref[...] * _sigmoid(hb_ref[...])
    buf_ref[0:CONV_HALO, :] = jnp.where(first, 0.0, halo)
    buf_ref[CONV_HALO:, :] = a_ref[...] * _sigmoid(b_ref[...])
    shift = CONV_HALO - (CONV_WIDTH - 1)

    def strip(s, carry):
        lane = pl.ds(pl.multiple_of(s * LANES, LANES), LANES)
        for rb in range(ts // CONV_ROWS):
            r0 = rb * CONV_ROWS + shift
            acc = jnp.zeros((CONV_ROWS, LANES), F32)
            for j in range(CONV_WIDTH):
                acc = acc + dww_ref[j:j + 1, lane] * buf_ref[r0 + j:r0 + j + CONV_ROWS, lane]
            cv_ref[rb * CONV_ROWS:(rb + 1) * CONV_ROWS, lane] = acc + dwb_ref[:, lane]
        return carry

    lax.fori_loop(0, c // LANES, strip, 0)

    def ln_rows(r, carry):
        rows = pl.ds(pl.multiple_of(r * 64, 64), 64)
        v = cv_ref[rows, :]
        mu = jnp.mean(v, axis=-1, keepdims=True)
        vc = v - mu
        y = vc * lax.rsqrt(jnp.mean(vc * vc, axis=-1, keepdims=True) + EPS)
        y = y * lnw_ref[...] + lnb_ref[...]
        hb16_ref[rows, :] = _silu(y).astype(BF16)
        return carry

    lax.fori_loop(0, ts // 64, ln_rows, 0)
    y_conv = _dot(hb16_ref[...], wco_ref[...])
    o_ref[...] = _sigmoid(gc_ref[...] + gb_ref[...]) * y_conv


def _conv_branch(proj, batch, seq, conv_dw_w, conv_dw_b, conv_ln_w, conv_ln_b, w_conv_out_bf16,
                 gate_bias0, ts=512):
    t = proj.shape[0]
    c = conv_dw_w.shape[1]
    nt = seq // ts
    hpt = ts // CONV_HALO

    def cur(col):
        return pl.BlockSpec((ts, c), lambda b, i: (b * nt + i, col))

    def halo(col):
        return pl.BlockSpec((CONV_HALO, c),
                            lambda b, i: (jnp.maximum((b * nt + i) * hpt - 1, 0), col))

    def row(n):
        return pl.BlockSpec((n, c), lambda b, i: (0, 0))

    return pl.pallas_call(
        _conv_kernel,
        grid=(batch, nt),
        in_specs=[cur(0), cur(1), halo(0), halo(1), cur(6),
                  row(CONV_WIDTH), row(1), row(1), row(1),
                  pl.BlockSpec((c, c), lambda b, i: (0, 0)), row(1)],
        out_specs=pl.BlockSpec((ts, c), lambda b, i: (b * nt + i, 0)),
        out_shape=jax.ShapeDtypeStruct((t, c), F32),
        scratch_shapes=[pltpu.VMEM((ts + CONV_HALO, c), F32), pltpu.VMEM((ts, c), F32),
                        pltpu.VMEM((ts, c), BF16)],
        compiler_params=_params(("parallel", "arbitrary")),
    )(proj, proj, proj, proj, proj, conv_dw_w, conv_dw_b, conv_ln_w, conv_ln_b,
      w_conv_out_bf16, gate_bias0)


DN_HEADS_PER_STEP = 4
DN_CARRY = 8
DN_PRE_ROWS = 128
DN_CHUNKS_PER_ITER = 2


def _mm_bf(a, b):
    return jnp.dot(a.astype(BF16), b.astype(BF16), preferred_element_type=F32)


def _mm_nt_bf(a, b):
    return lax.dot_general(a.astype(BF16), b.astype(BF16), (((1,), (1,)), ((), ())),
                           preferred_element_type=F32)


def _tri_inverse_minus_eye(l_mats, blk16, off32, off64):
    n = range(len(l_mats))
    ld = [jnp.where(blk16, l_mats[u], 0.0) for u in n]
    p2 = [_mm_bf(ld[u], ld[u]) for u in n]
    t = [_mm_bf(ld[u], p2[u]) for u in n]
    p4 = [_mm_bf(p2[u], p2[u]) for u in n]
    x = [p2[u] - ld[u] - t[u] for u in n]
    t = [_mm_bf(x[u], p4[u]) for u in n]
    p8 = [_mm_bf(p4[u], p4[u]) for u in n]
    x = [x[u] + p4[u] + t[u] for u in n]
    t = [_mm_bf(x[u], p8[u]) for u in n]
    x = [x[u] + p8[u] + t[u] for u in n]
    for off in (off32, off64):
        c = [jnp.where(off, l_mats[u], 0.0) for u in n]
        y = [_mm_bf(x[u], c[u]) for u in n]
        y = [c[u] + y[u] for u in n]
        t = [_mm_bf(y[u], x[u]) for u in n]
        x = [x[u] - (y[u] + t[u]) for u in n]
    return x


def _delta_kernel(q_ref, k_ref, v_ref, z_ref, a_ref, b_ref, cwq_ref, cwk_ref, cwv_ref,
                  alog_ref, dtb_ref, nw_ref, o_ref,
                  qraw_ref, kraw_ref, vraw_ref, qs_ref, ks_ref, vs_ref, g_ref, beta_ref,
                  pq_ref, n_ref, o0_ref, gl_ref, state_ref):
    ts = q_ref.shape[0]
    nh = q_ref.shape[1] // DN_HEAD_DIM
    dk = DN_HEAD_DIM

    @pl.when(pl.program_id(2) == 0)
    def _():
        zeros = jnp.zeros((DN_CARRY, q_ref.shape[1]), F32)
        qraw_ref[0:DN_CARRY, :] = zeros
        kraw_ref[0:DN_CARRY, :] = zeros
        vraw_ref[0:DN_CARRY, :] = zeros
        state_ref[...] = jnp.zeros_like(state_ref)

    for raw_ref, src_ref, cw_ref, dst_ref, norm in (
            (qraw_ref, q_ref, cwq_ref, qs_ref, True),
            (kraw_ref, k_ref, cwk_ref, ks_ref, True),
            (vraw_ref, v_ref, cwv_ref, vs_ref, False)):
        raw_ref[DN_CARRY:, :] = src_ref[...]
        for rb in range(ts // DN_PRE_ROWS):
            r0 = rb * DN_PRE_ROWS + DN_CARRY - (SHORT_CONV - 1)
            acc = jnp.zeros((DN_PRE_ROWS, raw_ref.shape[1]), F32)
            for j in range(SHORT_CONV):
                acc = acc + cw_ref[j:j + 1, :] * raw_ref[r0 + j:r0 + j + DN_PRE_ROWS, :]
            acc = _silu(acc)
            rows = slice(rb * DN_PRE_ROWS, (rb + 1) * DN_PRE_ROWS)
            if norm:
                for h in range(nh):
                    cols = slice(h * DN_HEAD_DIM, (h + 1) * DN_HEAD_DIM)
                    ah = acc[:, cols]
                    dst_ref[rows, cols] = ah * lax.rsqrt(
                        jnp.sum(ah * ah, axis=-1, keepdims=True) + EPS)
            else:
                dst_ref[rows, :] = acc
        raw_ref[0:DN_CARRY, :] = raw_ref[ts:ts + DN_CARRY, :]
    g_ref[...] = -jnp.exp(alog_ref[...]) * jax.nn.softplus(a_ref[...] + dtb_ref[...])
    beta_ref[...] = _sigmoid(b_ref[...])

    rows_all = nh * CHUNK
    ri = lax.broadcasted_iota(jnp.int32, (rows_all, rows_all), 0)
    ci = lax.broadcasted_iota(jnp.int32, (rows_all, rows_all), 1)
    same_head = (ri // CHUNK) == (ci // CHUNK)
    causal = jnp.logical_and(same_head, ri >= ci)
    strict = jnp.logical_and(same_head, ri > ci)
    blk16 = (ri // 16) == (ci // 16)
    off32 = jnp.logical_and((ri // 32) == (ci // 32), (ri // 16) != (ci // 16))
    off64 = jnp.logical_and(same_head, (ri // 32) != (ci // 32))
    ti = lax.broadcasted_iota(jnp.int32, (CHUNK, CHUNK), 0)
    tj = lax.broadcasted_iota(jnp.int32, (CHUNK, CHUNK), 1)
    tri_incl = jnp.where(ti >= tj, 1.0, 0.0).astype(BF16)
    scale = dk ** -0.5
    units = range(DN_CHUNKS_PER_ITER)

    def stack_heads(x):
        return jnp.concatenate([x[:, h * dk:(h + 1) * dk] for h in range(nh)], axis=0)

    def local_body(it, carry):
        rows = [pl.ds(pl.multiple_of((it * DN_CHUNKS_PER_ITER + u) * CHUNK, CHUNK), CHUNK)
                for u in units]
        g_all = [g_ref[rows[u], :] for u in units]
        g_hi = [g_all[u].astype(BF16) for u in units]
        g_lo = [(g_all[u] - g_hi[u].astype(F32)).astype(BF16) for u in units]
        gcum_all = [jnp.dot(tri_incl, g_hi[u], preferred_element_type=F32)
                    + jnp.dot(tri_incl, g_lo[u], preferred_element_type=F32) for u in units]
        gcum = [stack_heads(gcum_all[u]) for u in units]
        g_last = [stack_heads(jnp.broadcast_to(gcum_all[u][CHUNK - 1:CHUNK, :],
                                               gcum_all[u].shape)) for u in units]
        qc = [stack_heads(qs_ref[rows[u], :]) * scale for u in units]
        kc = [stack_heads(ks_ref[rows[u], :]) for u in units]
        beta = [stack_heads(beta_ref[rows[u], :]) for u in units]
        g_row = [jnp.broadcast_to(gcum[u].T[0:1, :], (rows_all, rows_all)) for u in units]
        g_col = [jnp.concatenate([gcum[u]] * (rows_all // dk), axis=1) for u in units]
        decay = [jnp.where(causal, jnp.exp(jnp.where(causal, g_col[u] - g_row[u], 0.0)), 0.0)
                 for u in units]
        eg = [jnp.exp(gcum[u]) for u in units]
        kb = [kc[u] * beta[u] for u in units]
        rhs = [jnp.concatenate([stack_heads(vs_ref[rows[u], :]) * beta[u], kb[u] * eg[u]], axis=1)
               for u in units]
        kk = [_mm_nt_bf(kb[u], kc[u]) for u in units]
        qk = [_mm_nt_bf(qc[u], kc[u]) for u in units]
        l_mat = [jnp.where(strict, kk[u] * decay[u], 0.0) for u in units]
        x_inv = _tri_inverse_minus_eye(l_mat, blk16, off32, off64)
        t = [_mm_bf(x_inv[u], rhs[u]) for u in units]
        uw = [rhs[u] + t[u] for u in units]
        qk = [qk[u] * decay[u] for u in units]
        k_dec = [kc[u] * jnp.exp(g_last[u] - gcum[u]) for u in units]
        qk_uw = [_mm_bf(qk[u], uw[u]) for u in units]
        qe = [qc[u] * eg[u] - qk_uw[u][:, dk:] for u in units]
        for u in units:
            for h in range(nh):
                hr = slice(h * CHUNK, (h + 1) * CHUNK)
                slot = (it * DN_CHUNKS_PER_ITER + u) * nh + h
                kd_uw = _mm_bf(k_dec[u][hr, :].T, uw[u][hr, :])
                pq_ref[slot, 0:dk, :] = kd_uw[:, dk:].astype(BF16)
                pq_ref[slot, dk:, :] = qe[u][hr, :].astype(BF16)
                n_ref[slot] = kd_uw[:, :dk]
                o0_ref[slot] = qk_uw[u][hr, :dk]
                gl_ref[slot] = jnp.exp(g_last[u][h * CHUNK:h * CHUNK + SUBLANES, :])
        return carry

    lax.fori_loop(0, ts // (CHUNK * DN_CHUNKS_PER_ITER), local_body, 0)

    def chain_body(cidx, carry):
        rows = pl.ds(pl.multiple_of(cidx * CHUNK, CHUNK), CHUNK)
        for h in range(nh):
            cols = slice(h * dk, (h + 1) * dk)
            slot = cidx * nh + h
            state = state_ref[h]
            pq_s = jnp.dot(pq_ref[slot], state.astype(BF16), preferred_element_type=F32)
            state_ref[h] = state * gl_ref[slot][0:1, :] - pq_s[0:dk, :] + n_ref[slot]
            out = pq_s[dk:, :] + o0_ref[slot]
            y = out * lax.rsqrt(jnp.mean(out * out, axis=-1, keepdims=True) + EPS) * nw_ref[...]
            o_ref[rows, cols] = (y * _silu(z_ref[rows, cols])).astype(o_ref.dtype)
        return carry

    lax.fori_loop(0, ts // CHUNK, chain_body, 0)


def _delta_branch(proj, batch, seq, dn_conv_w, alog_rep, dtb_rep, dn_norm_w, ts=512):
    t = proj.shape[0]
    hb = DN_HEADS_PER_STEP
    wcols = hb * DN_HEAD_DIM
    dn_dim = DN_HEADS * DN_HEAD_DIM
    nt = seq // ts
    per_group = dn_dim // wcols
    slots = (ts // CHUNK) * hb

    def col(group):
        return pl.BlockSpec((ts, wcols), lambda b, h, s: (b * nt + s, group * per_group + h))

    def cw(group):
        return pl.BlockSpec((SHORT_CONV, wcols), lambda b, h, s: (0, group * per_group + h))

    head_row = pl.BlockSpec((1, wcols), lambda b, h, s: (0, h))
    return pl.pallas_call(
        _delta_kernel,
        grid=(batch, DN_HEADS // hb, nt),
        in_specs=[col(2), col(3), col(4), col(5), col(8), col(9), cw(0), cw(1), cw(2),
                  head_row, head_row, pl.BlockSpec((1, DN_HEAD_DIM), lambda b, h, s: (0, 0))],
        out_specs=pl.BlockSpec((ts, wcols), lambda b, h, s: (b * nt + s, h)),
        out_shape=jax.ShapeDtypeStruct((t, dn_dim), BF16),
        scratch_shapes=[pltpu.VMEM((ts + DN_CARRY, wcols), F32)] * 3
                       + [pltpu.VMEM((ts, wcols), F32)] * 5
                       + [pltpu.VMEM((slots, DN_HEAD_DIM + CHUNK, DN_HEAD_DIM), BF16),
                          pltpu.VMEM((slots, DN_HEAD_DIM, DN_HEAD_DIM), F32),
                          pltpu.VMEM((slots, CHUNK, DN_HEAD_DIM), F32),
                          pltpu.VMEM((slots, SUBLANES, DN_HEAD_DIM), F32),
                          pltpu.VMEM((hb, DN_HEAD_DIM, DN_HEAD_DIM), F32)],
        compiler_params=_params(("parallel", "parallel", "arbitrary")),
    )(proj, proj, proj, proj, proj, proj, dn_conv_w, dn_conv_w, dn_conv_w,
      alog_rep, dtb_rep, dn_norm_w)


def _merge_kernel(x_ref, mc_ref, o_ref, gd_ref, gb_ref, wdn_ref, wout_ref, nw_ref, wq_ref,
                  h_ref, xn_ref, q_ref):
    y_dn = _dot(o_ref[...], wdn_ref[...])
    merged = mc_ref[...] + _sigmoid(gd_ref[...] + gb_ref[...]) * y_dn
    h = x_ref[...] + _dot(merged.astype(BF16), wout_ref[...])
    h_ref[...] = h
    y = h * lax.rsqrt(jnp.mean(h * h, axis=-1, keepdims=True) + EPS) * nw_ref[...]
    xn = y.astype(BF16)
    xn_ref[...] = xn
    q_ref[...] = _dot(xn, wq_ref[...])


def _merge(x2d, mc, o_gated, proj, gate_bias1, w_dn_out_bf16, w_out_bf16, norm_ffn_w,
           w_query_bf16, tm=512):
    t, d = x2d.shape
    nq = w_query_bf16.shape[1]
    tile = pl.BlockSpec((tm, d), lambda i: (i, 0))
    row = pl.BlockSpec((1, d), lambda i: (0, 0))
    full = pl.BlockSpec((d, d), lambda i: (0, 0))
    return pl.pallas_call(
        _merge_kernel,
        grid=(t // tm,),
        in_specs=[tile, tile, tile, pl.BlockSpec((tm, d), lambda i: (i, 7)), row, full, full,
                  row, pl.BlockSpec((d, nq), lambda i: (0, 0))],
        out_specs=[tile, tile, pl.BlockSpec((tm, nq), lambda i: (i, 0))],
        out_shape=[jax.ShapeDtypeStruct((t, d), F32), jax.ShapeDtypeStruct((t, d), BF16),
                   jax.ShapeDtypeStruct((t, nq), F32)],
        compiler_params=_params(("parallel",)),
    )(x2d, mc, o_gated, proj, gate_bias1, w_dn_out_bf16, w_out_bf16, norm_ffn_w, w_query_bf16)


def _top16(scores, key):
    rank = jnp.full(scores.shape, float(PEER_TOPK), F32)
    tops = []
    work = scores
    for r in range(PEER_TOPK):
        m = jnp.max(work, axis=-2, keepdims=True)
        first = jnp.min(jnp.where(work == m, key, jnp.int32(2 ** 30)), axis=-2, keepdims=True)
        hit = key == first
        rank = jnp.where(hit, float(r), rank)
        work = jnp.where(hit, -jnp.inf, work)
        tops.append(m)
    return rank, tops


CAND_SPLIT = 8
CAND_ROWS = PEER_TOPK + (CAND_SPLIT - 1) * CAND_SPLIT + (PEER_TOPK - CAND_SPLIT)


def _route_kernel(q_ref, keys_ref, e1_ref, cnt1_ref, e2_ref, rank2_ref):
    tt = q_ref.shape[0]
    key12 = lax.broadcasted_iota(jnp.int32, (2, N_KEYS, tt), 1)
    r = lax.broadcasted_iota(jnp.int32, (CAND_ROWS, tt), 0)
    mid = r - PEER_TOPK
    tail0 = PEER_TOPK + (CAND_SPLIT - 1) * CAND_SPLIT
    cand_key = jnp.where(
        r < PEER_TOPK, r,
        jnp.where(r < tail0,
                  (1 + jnp.right_shift(mid, 3)) * PEER_TOPK + jnp.bitwise_and(mid, CAND_SPLIT - 1),
                  (CAND_SPLIT + r - tail0) * PEER_TOPK))

    def head(h, carry):
        def scores(p):
            cols = pl.ds(pl.multiple_of((2 * h + p) * PEER_HALF, PEER_HALF), PEER_HALF)
            return _dot_nt(keys_ref[2 * h + p], q_ref[:, cols], precision=HIGHEST)

        s12 = jnp.stack([scores(0), scores(1)])
        rank12, tops = _top16(s12, key12)
        top1 = jnp.concatenate([t[0] for t in tops], axis=0)
        top2 = jnp.concatenate([t[1] for t in tops], axis=0)
        cand = jnp.concatenate(
            [top1[0:1, :] + top2]
            + [top1[k:k + 1, :] + top2[0:CAND_SPLIT, :] for k in range(1, CAND_SPLIT)]
            + [top1[CAND_SPLIT:, :] + top2[0:1, :]], axis=0)
        crank, _ = _top16(cand, cand_key)
        sel = crank < float(PEER_TOPK)
        z = jnp.sum(jnp.where(sel, jnp.exp(cand - cand[0:1, :]), 0.0), axis=0, keepdims=True)
        hits = jnp.where(sel, 1.0, 0.0)
        rank1 = rank12[0]
        cnt1 = jnp.zeros((N_KEYS, tt), F32)
        for k in range(PEER_TOPK):
            if k == 0:
                ck = jnp.sum(hits[0:PEER_TOPK, :], axis=0, keepdims=True)
            elif k < CAND_SPLIT:
                lo = PEER_TOPK + (k - 1) * CAND_SPLIT
                ck = jnp.sum(hits[lo:lo + CAND_SPLIT, :], axis=0, keepdims=True)
            else:
                ck = hits[tail0 + k - CAND_SPLIT:tail0 + k - CAND_SPLIT + 1, :]
            cnt1 = jnp.where(rank1 == float(k), ck, cnt1)
        e1_ref[h] = jnp.exp(s12[0] - top1[0:1, :])
        cnt1_ref[h] = cnt1
        e2_ref[h] = (jnp.exp(s12[1] - top2[0:1, :]) / z).astype(e2_ref.dtype)
        rank2_ref[h] = rank12[1].astype(rank2_ref.dtype)
        return carry

    lax.fori_loop(0, PEER_HEADS, head, 0)


def _route(q, keys, tt=128):
    t = q.shape[0]
    spec = pl.BlockSpec((PEER_HEADS, N_KEYS, tt), lambda i: (0, 0, i))
    shape = (PEER_HEADS, N_KEYS, t)
    return pl.pallas_call(
        _route_kernel,
        grid=(t // tt,),
        in_specs=[pl.BlockSpec((tt, q.shape[1]), lambda i: (i, 0)),
                  pl.BlockSpec(keys.shape, lambda i: (0, 0, 0))],
        out_specs=[spec] * 4,
        out_shape=[jax.ShapeDtypeStruct(shape, F32), jax.ShapeDtypeStruct(shape, F32),
                   jax.ShapeDtypeStruct(shape, BF16), jax.ShapeDtypeStruct(shape, BF16)],
        compiler_params=_params(("parallel",)),
    )(q, keys)


PEER_ROWS_PER_STEP = SUBLANES


PEER_SUB_TOKENS = 256


def _peer_kernel(xn_ref, down_ref, upt_ref, e1_ref, cnt1_ref, e2_ref, rank2_ref, h_ref, nw_ref,
                 o_ref, acc_ref, act_ref, ct_ref):
    j = pl.program_id(1)
    tt = xn_ref.shape[0]

    @pl.when(j == 0)
    def _():
        acc_ref[...] = jnp.zeros_like(acc_ref)

    nsub = tt // PEER_SUB_TOKENS

    def expert_acts(st):
        toks = slice(st * PEER_SUB_TOKENS, (st + 1) * PEER_SUB_TOKENS)
        act_ref[st] = _dot_nt(down_ref[...], xn_ref[toks, :])

    expert_acts(0)
    for st in range(nsub):
        if st + 1 < nsub:
            expert_acts(st + 1)
        for ib in range(PEER_ROWS_PER_STEP):
            rows = slice(ib * N_KEYS, (ib + 1) * N_KEYS)
            for half in range(PEER_SUB_TOKENS // LANES):
                sub = slice(half * LANES, (half + 1) * LANES)
                lanes = slice(st * PEER_SUB_TOKENS + half * LANES,
                              st * PEER_SUB_TOKENS + (half + 1) * LANES)
                wsum = jnp.zeros((N_KEYS, LANES), BF16)
                for h in range(PEER_HEADS):
                    p = e1_ref[h, ib:ib + 1, lanes].astype(BF16)
                    c = cnt1_ref[h, ib:ib + 1, lanes].astype(BF16)
                    keys = slice(h * N_KEYS, (h + 1) * N_KEYS)
                    wsum = wsum + jnp.where(rank2_ref[keys, lanes] < c, e2_ref[keys, lanes],
                                            jnp.zeros((), BF16)) * p
                ct_ref[st, rows, sub] = wsum * _gelu(act_ref[st, rows, sub].astype(BF16))
        toks = slice(st * PEER_SUB_TOKENS, (st + 1) * PEER_SUB_TOKENS)
        acc_ref[:, toks] += _dot(upt_ref[...], ct_ref[st])

    @pl.when(j == pl.num_programs(1) - 1)
    def _():
        h = h_ref[...] + acc_ref[...].T
        o_ref[...] = h * lax.rsqrt(jnp.mean(h * h, axis=-1, keepdims=True) + EPS) * nw_ref[...]


def _peer(xn2, down_bf16, upt_bf16, e1, cnt1, e2, rank2, h1, final_norm_w, tt=1024):
    t, d = xn2.shape
    tt = min(tt, t)
    eb = PEER_ROWS_PER_STEP * N_KEYS
    n_exp = down_bf16.shape[0]
    nsub = tt // PEER_SUB_TOKENS
    fac_i = pl.BlockSpec((PEER_HEADS, PEER_ROWS_PER_STEP, tt), lambda i, j: (0, j, i))
    fac_j = pl.BlockSpec((PEER_HEADS * N_KEYS, tt), lambda i, j: (0, i))
    e2 = e2.reshape(PEER_HEADS * N_KEYS, t)
    rank2 = rank2.reshape(PEER_HEADS * N_KEYS, t)
    tile = pl.BlockSpec((tt, d), lambda i, j: (i, 0))
    return pl.pallas_call(
        _peer_kernel,
        grid=(t // tt, n_exp // eb),
        in_specs=[tile, pl.BlockSpec((eb, d), lambda i, j: (j, 0)),
                  pl.BlockSpec((d, eb), lambda i, j: (0, j)), fac_i, fac_i, fac_j, fac_j, tile,
                  pl.BlockSpec((1, d), lambda i, j: (0, 0))],
        out_specs=tile,
        out_shape=jax.ShapeDtypeStruct((t, d), F32),
        scratch_shapes=[pltpu.VMEM((d, tt), F32), pltpu.VMEM((nsub, eb, PEER_SUB_TOKENS), F32),
                        pltpu.VMEM((nsub, eb, PEER_SUB_TOKENS), BF16)],
        compiler_params=_params(("parallel", "arbitrary")),
    )(xn2, down_bf16, upt_bf16, e1, cnt1, e2, rank2, h1, final_norm_w)


def _layer(h2d, batch, seq, norm_mix_w, w_in, gate_bias, conv_dw_w, conv_dw_b, conv_ln_w,
           conv_ln_b, w_conv_out, dn_conv_w, dn_a_log, dn_dt_bias, dn_norm_w, w_dn_out, w_out,
           norm_ffn_w, peer_w_query, peer_sub_keys, peer_down, peer_up, out_norm_w):
    d = h2d.shape[1]
    conv_dim = conv_dw_w.shape[1]
    dn_dim = DN_HEADS * DN_HEAD_DIM
    o_qkv = 2 * conv_dim
    o_z = o_qkv + 3 * dn_dim
    o_a = o_z + dn_dim
    o_b = o_a + DN_HEADS
    o_gc = o_b + DN_HEADS
    w1 = jnp.concatenate([w_in[:, :o_a], w_in[:, o_gc:],
                          jnp.repeat(w_in[:, o_a:o_b], DN_HEAD_DIM, axis=1),
                          jnp.repeat(w_in[:, o_b:o_gc], DN_HEAD_DIM, axis=1)], axis=1).astype(BF16)
    proj = _inproj(h2d, norm_mix_w.reshape(1, d), w1)
    mc = _conv_branch(proj, batch, seq, conv_dw_w, conv_dw_b.reshape(1, -1),
                      conv_ln_w.reshape(1, -1), conv_ln_b.reshape(1, -1),
                      w_conv_out.astype(BF16), gate_bias[0].reshape(1, d))
    o_gated = _delta_branch(proj, batch, seq, dn_conv_w,
                            jnp.repeat(dn_a_log, DN_HEAD_DIM).reshape(1, dn_dim),
                            jnp.repeat(dn_dt_bias, DN_HEAD_DIM).reshape(1, dn_dim),
                            dn_norm_w.reshape(1, DN_HEAD_DIM))
    h1, xn2, q = _merge(h2d, mc, o_gated, proj, gate_bias[1].reshape(1, d),
                        w_dn_out.astype(BF16), w_out.astype(BF16), norm_ffn_w.reshape(1, d),
                        peer_w_query.astype(BF16))
    keys = peer_sub_keys.reshape(PEER_HEADS * 2, N_KEYS, PEER_HALF)
    e1, cnt1, e2, rank2 = _route(q, keys)
    return _peer(xn2, peer_down.astype(BF16), peer_up.T.astype(BF16), e1, cnt1, e2, rank2, h1,
                 out_norm_w.reshape(1, d))


def kernel(x, norm_mix_w, w_in, gate_bias, conv_dw_w, conv_dw_b, conv_ln_w, conv_ln_b,
           w_conv_out, dn_conv_w, dn_a_log, dn_dt_bias, dn_norm_w, w_dn_out, w_out, norm_ffn_w,
           peer_w_query, peer_sub_keys, peer_down, peer_up, final_norm_w):
    batch, seq, d = x.shape
    depth = w_in.shape[0]
    assert depth == 1, "the final RMSNorm is fused into the (single) layer's last stage"
    out = _layer(x.reshape(batch * seq, d), batch, seq, norm_mix_w[0], w_in[0], gate_bias[0],
                 conv_dw_w[0], conv_dw_b[0], conv_ln_w[0], conv_ln_b[0], w_conv_out[0],
                 dn_conv_w[0], dn_a_log[0], dn_dt_bias[0], dn_norm_w[0], w_dn_out[0], w_out[0],
                 norm_ffn_w[0], peer_w_query[0], peer_sub_keys[0], peer_down[0], peer_up[0],
                 final_norm_w)
    return out.reshape(batch, seq, d)
```

```python
import functools

import jax
import jax.numpy as jnp
from jax import lax
from jax.experimental import pallas as pl
from jax.experimental.pallas import tpu as pltpu

F32 = jnp.float32
BF16 = jnp.bfloat16
EPS = 1e-6
HIGHEST = lax.Precision.HIGHEST

CONV_WIDTH = 31
SHORT_CONV = 4
DN_HEADS = 8
DN_HEAD_DIM = 128
CHUNK = 64
N_KEYS = 128
PEER_HEADS = 8
PEER_HALF = 128
PEER_TOPK = 16

LANES = 128
SUBLANES = 8
VMEM_LIMIT = 56 * 1024 * 1024


def _params(semantics):
    return pltpu.CompilerParams(dimension_semantics=semantics, vmem_limit_bytes=VMEM_LIMIT)


def _dot(a, b, precision=None):
    return jnp.dot(a, b, preferred_element_type=F32, precision=precision)


def _dot_nt(a, b, precision=None):
    return lax.dot_general(a, b, (((1,), (1,)), ((), ())), preferred_element_type=F32,
                           precision=precision)


def _sigmoid(x):
    return jax.nn.sigmoid(x)


def _silu(x):
    return x * jax.nn.sigmoid(x)


def _gelu(x):
    return 0.5 * x * (1.0 + lax.erf(x * (2.0 ** -0.5)))


def _inproj_kernel(x_ref, nw_ref, w_ref, o_ref, xn_ref):
    @pl.when(pl.program_id(1) == 0)
    def _():
        x = x_ref[...]
        y = x * lax.rsqrt(jnp.mean(x * x, axis=-1, keepdims=True) + EPS)
        xn_ref[...] = (y * nw_ref[...]).astype(BF16)

    o_ref[...] = _dot(xn_ref[...], w_ref[...])


def _inproj(x2d, norm_w, w_bf16, tm=1024, tn=512):
    t, d = x2d.shape
    n = w_bf16.shape[1]
    return pl.pallas_call(
        _inproj_kernel,
        grid=(t // tm, n // tn),
        in_specs=[pl.BlockSpec((tm, d), lambda i, j: (i, 0)),
                  pl.BlockSpec((1, d), lambda i, j: (0, 0)),
                  pl.BlockSpec((d, tn), lambda i, j: (0, j))],
        out_specs=pl.BlockSpec((tm, tn), lambda i, j: (i, j)),
        out_shape=jax.ShapeDtypeStruct((t, n), F32),
        scratch_shapes=[pltpu.VMEM((tm, d), BF16)],
        compiler_params=_params(("parallel", "arbitrary")),
    )(x2d, norm_w, w_bf16)


CONV_HALO = 32
CONV_ROWS = 128


def _conv_kernel(a_ref, b_ref, ha_ref, hb_ref, gc_ref, dww_ref, dwb_ref, lnw_ref, lnb_ref,
                 wco_ref, gb_ref, o_ref, buf_ref, cv_ref, hb16_ref, sh_ref):
    ts, c = a_ref.shape
    first = pl.program_id(1) == 0
    halo = ha_ref[...] * _sigmoid(hb_ref[...])
    buf_ref[0:CONV_HALO, :] = jnp.where(first, 0.0, halo)
    buf_ref[CONV_HALO:, :] = a_ref[...] * _sigmoid(b_ref[...])
    shift = CONV_HALO - (CONV_WIDTH - 1)

    def strip(s, carry):
        lane = pl.ds(pl.multiple_of(s * LANES, LANES), LANES)
        for rb in range(ts // CONV_ROWS):
            window = buf_ref[rb * CONV_ROWS:rb * CONV_ROWS + CONV_ROWS + CONV_HALO, lane]
            acc = jnp.zeros((CONV_ROWS, LANES), F32)
            for phase in range(SUBLANES):
                taps = [j for j in range(CONV_WIDTH) if (shift + j) % SUBLANES == phase]
                span = max((shift + j) // SUBLANES for j in taps) * SUBLANES + CONV_ROWS
                sh_ref[phase, 0:span, :] = window[phase:phase + span, :]
                for j in taps:
                    a = (shift + j) // SUBLANES * SUBLANES
                    acc = acc + dww_ref[j:j + 1, lane] * sh_ref[phase, a:a + CONV_ROWS, :]
            cv_ref[rb * CONV_ROWS:(rb + 1) * CONV_ROWS, lane] = acc + dwb_ref[:, lane]
        return carry

    lax.fori_loop(0, c // LANES, strip, 0)

    def ln_rows(r, carry):
        rows = pl.ds(pl.multiple_of(r * 64, 64), 64)
        v = cv_ref[rows, :]
        mu = jnp.mean(v, axis=-1, keepdims=True)
        vc = v - mu
        y = vc * lax.rsqrt(jnp.mean(vc * vc, axis=-1, keepdims=True) + EPS)
        y = y * lnw_ref[...] + lnb_ref[...]
        hb16_ref[rows, :] = _silu(y).astype(BF16)
        return carry

    lax.fori_loop(0, ts // 64, ln_rows, 0)
    y_conv = _dot(hb16_ref[...], wco_ref[...])
    o_ref[...] = _sigmoid(gc_ref[...] + gb_ref[...]) * y_conv


def _conv_branch(proj, batch, seq, conv_dw_w, conv_dw_b, conv_ln_w, conv_ln_b, w_conv_out_bf16,
                 gate_bias0, ts=512):
    t = proj.shape[0]
    c = conv_dw_w.shape[1]
    nt = seq // ts
    hpt = ts // CONV_HALO

    def cur(col):
        return pl.BlockSpec((ts, c), lambda b, i: (b * nt + i, col))

    def halo(col):
        return pl.BlockSpec((CONV_HALO, c),
                            lambda b, i: (jnp.maximum((b * nt + i) * hpt - 1, 0), col))

    def row(n):
        return pl.BlockSpec((n, c), lambda b, i: (0, 0))

    return pl.pallas_call(
        _conv_kernel,
        grid=(batch, nt),
        in_specs=[cur(0), cur(1), halo(0), halo(1), cur(6),
                  row(CONV_WIDTH), row(1), row(1), row(1),
                  pl.BlockSpec((c, c), lambda b, i: (0, 0)), row(1)],
        out_specs=pl.BlockSpec((ts, c), lambda b, i: (b * nt + i, 0)),
        out_shape=jax.ShapeDtypeStruct((t, c), F32),
        scratch_shapes=[pltpu.VMEM((ts + CONV_HALO, c), F32), pltpu.VMEM((ts, c), F32),
                        pltpu.VMEM((ts, c), BF16),
                        pltpu.VMEM((SUBLANES, CONV_ROWS + CONV_HALO, LANES), F32)],
        compiler_params=_params(("parallel", "arbitrary")),
    )(proj, proj, proj, proj, proj, conv_dw_w, conv_dw_b, conv_ln_w, conv_ln_b,
      w_conv_out_bf16, gate_bias0)


DN_HEADS_PER_STEP = 4
DN_CARRY = 8
DN_PRE_ROWS = 128
DN_CHUNKS_PER_ITER = 2


def _mm_bf(a, b):
    return jnp.dot(a.astype(BF16), b.astype(BF16), preferred_element_type=F32)


def _mm_nt_bf(a, b):
    return lax.dot_general(a.astype(BF16), b.astype(BF16), (((1,), (1,)), ((), ())),
                           preferred_element_type=F32)


def _tri_inverse_minus_eye(l_mats, blk16, off32, off64):
    n = range(len(l_mats))
    ld = [jnp.where(blk16, l_mats[u], 0.0) for u in n]
    p2 = [_mm_bf(ld[u], ld[u]) for u in n]
    t = [_mm_bf(ld[u], p2[u]) for u in n]
    p4 = [_mm_bf(p2[u], p2[u]) for u in n]
    x = [p2[u] - ld[u] - t[u] for u in n]
    t = [_mm_bf(x[u], p4[u]) for u in n]
    p8 = [_mm_bf(p4[u], p4[u]) for u in n]
    x = [x[u] + p4[u] + t[u] for u in n]
    t = [_mm_bf(x[u], p8[u]) for u in n]
    x = [x[u] + p8[u] + t[u] for u in n]
    for off in (off32, off64):
        c = [jnp.where(off, l_mats[u], 0.0) for u in n]
        y = [_mm_bf(x[u], c[u]) for u in n]
        y = [c[u] + y[u] for u in n]
        t = [_mm_bf(y[u], x[u]) for u in n]
        x = [x[u] - (y[u] + t[u]) for u in n]
    return x


def _delta_kernel(q_ref, k_ref, v_ref, z_ref, a_ref, b_ref, cwq_ref, cwk_ref, cwv_ref,
                  alog_ref, dtb_ref, nw_ref, o_ref,
                  qraw_ref, kraw_ref, vraw_ref, qs_ref, ks_ref, vs_ref, g_ref, beta_ref,
                  pq_ref, n_ref, o0_ref, gl_ref, state_ref, sh_ref):
    ts = q_ref.shape[0]
    nh = q_ref.shape[1] // DN_HEAD_DIM
    dk = DN_HEAD_DIM

    @pl.when(pl.program_id(2) == 0)
    def _():
        zeros = jnp.zeros((DN_CARRY, q_ref.shape[1]), F32)
        qraw_ref[0:DN_CARRY, :] = zeros
        kraw_ref[0:DN_CARRY, :] = zeros
        vraw_ref[0:DN_CARRY, :] = zeros
        state_ref[...] = jnp.zeros_like(state_ref)

    for raw_ref, src_ref, cw_ref, dst_ref, norm in (
            (qraw_ref, q_ref, cwq_ref, qs_ref, True),
            (kraw_ref, k_ref, cwk_ref, ks_ref, True),
            (vraw_ref, v_ref, cwv_ref, vs_ref, False)):
        raw_ref[DN_CARRY:, :] = src_ref[...]
        raw_all = raw_ref[...]
        for j in range(SHORT_CONV - 1):
            off = SHORT_CONV - 1 - j
            sh_ref[j, off:off + ts + DN_CARRY, :] = raw_all
        for rb in range(ts // DN_PRE_ROWS):
            blk = slice(DN_CARRY + rb * DN_PRE_ROWS, DN_CARRY + (rb + 1) * DN_PRE_ROWS)
            acc = cw_ref[SHORT_CONV - 1:SHORT_CONV, :] * raw_ref[blk, :]
            for j in range(SHORT_CONV - 1):
                acc = acc + cw_ref[j:j + 1, :] * sh_ref[j, blk, :]
            acc = _silu(acc)
            rows = slice(rb * DN_PRE_ROWS, (rb + 1) * DN_PRE_ROWS)
            if norm:
                for h in range(nh):
                    cols = slice(h * DN_HEAD_DIM, (h + 1) * DN_HEAD_DIM)
                    ah = acc[:, cols]
                    dst_ref[rows, cols] = ah * lax.rsqrt(
                        jnp.sum(ah * ah, axis=-1, keepdims=True) + EPS)
            else:
                dst_ref[rows, :] = acc
        raw_ref[0:DN_CARRY, :] = raw_ref[ts:ts + DN_CARRY, :]
    g_ref[...] = -jnp.exp(alog_ref[...]) * jax.nn.softplus(a_ref[...] + dtb_ref[...])
    beta_ref[...] = _sigmoid(b_ref[...])

    rows_all = nh * CHUNK
    ri = lax.broadcasted_iota(jnp.int32, (rows_all, rows_all), 0)
    ci = lax.broadcasted_iota(jnp.int32, (rows_all, rows_all), 1)
    same_head = (ri // CHUNK) == (ci // CHUNK)
    causal = jnp.logical_and(same_head, ri >= ci)
    strict = jnp.logical_and(same_head, ri > ci)
    blk16 = (ri // 16) == (ci // 16)
    off32 = jnp.logical_and((ri // 32) == (ci // 32), (ri // 16) != (ci // 16))
    off64 = jnp.logical_and(same_head, (ri // 32) != (ci // 32))
    ti = lax.broadcasted_iota(jnp.int32, (CHUNK, CHUNK), 0)
    tj = lax.broadcasted_iota(jnp.int32, (CHUNK, CHUNK), 1)
    tri_incl = jnp.where(ti >= tj, 1.0, 0.0).astype(BF16)
    scale = dk ** -0.5
    units = range(DN_CHUNKS_PER_ITER)

    def stack_heads(x):
        return jnp.concatenate([x[:, h * dk:(h + 1) * dk] for h in range(nh)], axis=0)

    def local_body(it, carry):
        rows = [pl.ds(pl.multiple_of((it * DN_CHUNKS_PER_ITER + u) * CHUNK, CHUNK), CHUNK)
                for u in units]
        g_all = [g_ref[rows[u], :] for u in units]
        g_hi = [g_all[u].astype(BF16) for u in units]
        g_lo = [(g_all[u] - g_hi[u].astype(F32)).astype(BF16) for u in units]
        gcum_all = [jnp.dot(tri_incl, g_hi[u], preferred_element_type=F32)
                    + jnp.dot(tri_incl, g_lo[u], preferred_element_type=F32) for u in units]
        gcum = [stack_heads(gcum_all[u]) for u in units]
        g_last = [stack_heads(jnp.broadcast_to(gcum_all[u][CHUNK - 1:CHUNK, :],
                                               gcum_all[u].shape)) for u in units]
        qc = [stack_heads(qs_ref[rows[u], :]) * scale for u in units]
        kc = [stack_heads(ks_ref[rows[u], :]) for u in units]
        beta = [stack_heads(beta_ref[rows[u], :]) for u in units]
        g_row = [jnp.broadcast_to(gcum[u].T[0:1, :], (rows_all, rows_all)) for u in units]
        g_col = [jnp.concatenate([gcum[u]] * (rows_all // dk), axis=1) for u in units]
        decay = [jnp.where(causal, jnp.exp(jnp.where(causal, g_col[u] - g_row[u], 0.0)), 0.0)
                 for u in units]
        eg = [jnp.exp(gcum[u]) for u in units]
        kb = [kc[u] * beta[u] for u in units]
        rhs = [jnp.concatenate([stack_heads(vs_ref[rows[u], :]) * beta[u], kb[u] * eg[u]], axis=1)
               for u in units]
        kk = [_mm_nt_bf(kb[u], kc[u]) for u in units]
        qk = [_mm_nt_bf(qc[u], kc[u]) for u in units]
        l_mat = [jnp.where(strict, kk[u] * decay[u], 0.0) for u in units]
        x_inv = _tri_inverse_minus_eye(l_mat, blk16, off32, off64)
        t = [_mm_bf(x_inv[u], rhs[u]) for u in units]
        uw = [rhs[u] + t[u] for u in units]
        qk = [qk[u] * decay[u] for u in units]
        k_dec = [kc[u] * jnp.exp(g_last[u] - gcum[u]) for u in units]
        qk_uw = [_mm_bf(qk[u], uw[u]) for u in units]
        qe = [qc[u] * eg[u] - qk_uw[u][:, dk:] for u in units]
        for u in units:
            for h in range(nh):
                hr = slice(h * CHUNK, (h + 1) * CHUNK)
                slot = (it * DN_CHUNKS_PER_ITER + u) * nh + h
                kd_uw = _mm_bf(k_dec[u][hr, :].T, uw[u][hr, :])
                pq_ref[slot, 0:dk, :] = kd_uw[:, dk:].astype(BF16)
                pq_ref[slot, dk:, :] = qe[u][hr, :].astype(BF16)
                n_ref[slot] = kd_uw[:, :dk]
                o0_ref[slot] = qk_uw[u][hr, :dk]
                gl_ref[slot] = jnp.exp(g_last[u][h * CHUNK:h * CHUNK + SUBLANES, :])
        return carry

    lax.fori_loop(0, ts // (CHUNK * DN_CHUNKS_PER_ITER), local_body, 0)

    def chain_body(cidx, carry):
        rows = pl.ds(pl.multiple_of(cidx * CHUNK, CHUNK), CHUNK)
        for h in range(nh):
            cols = slice(h * dk, (h + 1) * dk)
            slot = cidx * nh + h
            state = state_ref[h]
            pq_s = jnp.dot(pq_ref[slot], state.astype(BF16), preferred_element_type=F32)
            state_ref[h] = state * gl_ref[slot][0:1, :] - pq_s[0:dk, :] + n_ref[slot]
            out = pq_s[dk:, :] + o0_ref[slot]
            y = out * lax.rsqrt(jnp.mean(out * out, axis=-1, keepdims=True) + EPS) * nw_ref[...]
            o_ref[rows, cols] = (y * _silu(z_ref[rows, cols])).astype(o_ref.dtype)
        return carry

    lax.fori_loop(0, ts // CHUNK, chain_body, 0)


def _delta_branch(proj, batch, seq, dn_conv_w, alog_rep, dtb_rep, dn_norm_w, ts=512):
    t = proj.shape[0]
    hb = DN_HEADS_PER_STEP
    wcols = hb * DN_HEAD_DIM
    dn_dim = DN_HEADS * DN_HEAD_DIM
    nt = seq // ts
    per_group = dn_dim // wcols
    slots = (ts // CHUNK) * hb

    def col(group):
        return pl.BlockSpec((ts, wcols), lambda b, h, s: (b * nt + s, group * per_group + h))

    def cw(group):
        return pl.BlockSpec((SHORT_CONV, wcols), lambda b, h, s: (0, group * per_group + h))

    head_row = pl.BlockSpec((1, wcols), lambda b, h, s: (0, h))
    return pl.pallas_call(
        _delta_kernel,
        grid=(batch, DN_HEADS // hb, nt),
        in_specs=[col(2), col(3), col(4), col(5), col(8), col(9), cw(0), cw(1), cw(2),
                  head_row, head_row, pl.BlockSpec((1, DN_HEAD_DIM), lambda b, h, s: (0, 0))],
        out_specs=pl.BlockSpec((ts, wcols), lambda b, h, s: (b * nt + s, h)),
        out_shape=jax.ShapeDtypeStruct((t, dn_dim), BF16),
        scratch_shapes=[pltpu.VMEM((ts + DN_CARRY, wcols), F32)] * 3
                       + [pltpu.VMEM((ts, wcols), F32)] * 5
                       + [pltpu.VMEM((slots, DN_HEAD_DIM + CHUNK, DN_HEAD_DIM), BF16),
                          pltpu.VMEM((slots, DN_HEAD_DIM, DN_HEAD_DIM), F32),
                          pltpu.VMEM((slots, CHUNK, DN_HEAD_DIM), F32),
                          pltpu.VMEM((slots, SUBLANES, DN_HEAD_DIM), F32),
                          pltpu.VMEM((hb, DN_HEAD_DIM, DN_HEAD_DIM), F32),
                          pltpu.VMEM((SHORT_CONV - 1, ts + 2 * DN_CARRY, wcols), F32)],
        compiler_params=_params(("parallel", "parallel", "arbitrary")),
    )(proj, proj, proj, proj, proj, proj, dn_conv_w, dn_conv_w, dn_conv_w,
      alog_rep, dtb_rep, dn_norm_w)


def _merge_kernel(x_ref, mc_ref, o_ref, gd_ref, gb_ref, wdn_ref, wout_ref, nw_ref, wq_ref,
                  h_ref, xn_ref, q_ref):
    y_dn = _dot(o_ref[...], wdn_ref[...])
    merged = mc_ref[...] + _sigmoid(gd_ref[...] + gb_ref[...]) * y_dn
    h = x_ref[...] + _dot(merged.astype(BF16), wout_ref[...])
    h_ref[...] = h
    y = h * lax.rsqrt(jnp.mean(h * h, axis=-1, keepdims=True) + EPS) * nw_ref[...]
    xn = y.astype(BF16)
    xn_ref[...] = xn
    q_ref[...] = _dot(xn, wq_ref[...])


def _merge(x2d, mc, o_gated, proj, gate_bias1, w_dn_out_bf16, w_out_bf16, norm_ffn_w,
           w_query_bf16, tm=512):
    t, d = x2d.shape
    nq = w_query_bf16.shape[1]
    tile = pl.BlockSpec((tm, d), lambda i: (i, 0))
    row = pl.BlockSpec((1, d), lambda i: (0, 0))
    full = pl.BlockSpec((d, d), lambda i: (0, 0))
    return pl.pallas_call(
        _merge_kernel,
        grid=(t // tm,),
        in_specs=[tile, tile, tile, pl.BlockSpec((tm, d), lambda i: (i, 7)), row, full, full,
                  row, pl.BlockSpec((d, nq), lambda i: (0, 0))],
        out_specs=[tile, tile, pl.BlockSpec((tm, nq), lambda i: (i, 0))],
        out_shape=[jax.ShapeDtypeStruct((t, d), F32), jax.ShapeDtypeStruct((t, d), BF16),
                   jax.ShapeDtypeStruct((t, nq), F32)],
        compiler_params=_params(("parallel",)),
    )(x2d, mc, o_gated, proj, gate_bias1, w_dn_out_bf16, w_out_bf16, norm_ffn_w, w_query_bf16)


def _top16(scores, key):
    rank = jnp.full(scores.shape, float(PEER_TOPK), F32)
    tops = []
    work = scores
    for r in range(PEER_TOPK):
        m = jnp.max(work, axis=-2, keepdims=True)
        first = jnp.min(jnp.where(work == m, key, jnp.inf), axis=-2, keepdims=True)
        hit = key == first
        rank = jnp.where(hit, float(r), rank)
        work = jnp.where(hit, -jnp.inf, work)
        tops.append(m)
    return rank, tops


CAND_SPLIT = 8
CAND_ROWS = PEER_TOPK + (CAND_SPLIT - 1) * CAND_SPLIT + (PEER_TOPK - CAND_SPLIT)
ROUTE_HEADS_PER_ITER = 2


def _route_kernel(q_ref, keys_ref, e1_ref, cnt1_ref, e2_ref, rank2_ref):
    tt = q_ref.shape[0]
    ng = ROUTE_HEADS_PER_ITER
    key12 = lax.broadcasted_iota(jnp.int32, (2 * ng, N_KEYS, tt), 1).astype(F32)
    r = lax.broadcasted_iota(jnp.int32, (ng, CAND_ROWS, tt), 1)
    mid = r - PEER_TOPK
    tail0 = PEER_TOPK + (CAND_SPLIT - 1) * CAND_SPLIT
    cand_key = jnp.where(
        r < PEER_TOPK, r,
        jnp.where(r < tail0,
                  (1 + jnp.right_shift(mid, 3)) * PEER_TOPK + jnp.bitwise_and(mid, CAND_SPLIT - 1),
                  (CAND_SPLIT + r - tail0) * PEER_TOPK)).astype(F32)

    def head_group(it, carry):
        def scores(hp):
            cols = pl.ds(pl.multiple_of(hp * PEER_HALF, PEER_HALF), PEER_HALF)
            return _dot_nt(keys_ref[hp], q_ref[:, cols], precision=HIGHEST)

        s12 = jnp.stack([scores(2 * ng * it + i) for i in range(2 * ng)])
        rank12, tops = _top16(s12, key12)
        top = [jnp.concatenate([t[i] for t in tops], axis=0) for i in range(2 * ng)]
        cands = []
        for g in range(ng):
            top1, top2 = top[2 * g], top[2 * g + 1]
            cands.append(jnp.concatenate(
                [top1[0:1, :] + top2]
                + [top1[k:k + 1, :] + top2[0:CAND_SPLIT, :] for k in range(1, CAND_SPLIT)]
                + [top1[CAND_SPLIT:, :] + top2[0:1, :]], axis=0))
        cand = jnp.stack(cands)
        crank, _ = _top16(cand, cand_key)
        for g in range(ng):
            h = ng * it + g
            top1, top2 = top[2 * g], top[2 * g + 1]
            sel = crank[g] < float(PEER_TOPK)
            z = jnp.sum(jnp.where(sel, jnp.exp(cands[g] - cands[g][0:1, :]), 0.0), axis=0,
                        keepdims=True)
            hits = jnp.where(sel, 1.0, 0.0)
            rank1 = rank12[2 * g]
            cnt1 = jnp.zeros((N_KEYS, tt), F32)
            for k in range(PEER_TOPK):
                if k == 0:
                    ck = jnp.sum(hits[0:PEER_TOPK, :], axis=0, keepdims=True)
                elif k < CAND_SPLIT:
                    lo = PEER_TOPK + (k - 1) * CAND_SPLIT
                    ck = jnp.sum(hits[lo:lo + CAND_SPLIT, :], axis=0, keepdims=True)
                else:
                    ck = hits[tail0 + k - CAND_SPLIT:tail0 + k - CAND_SPLIT + 1, :]
                cnt1 = jnp.where(rank1 == float(k), ck, cnt1)
            e1_ref[h] = jnp.exp(s12[2 * g] - top1[0:1, :])
            cnt1_ref[h] = cnt1
            e2_ref[h] = pltpu.bitcast(
                (jnp.exp(s12[2 * g + 1] - top2[0:1, :]) / z).astype(BF16), jnp.uint32)
            rank2_ref[h] = pltpu.bitcast(rank12[2 * g + 1].astype(BF16), jnp.uint32)
        return carry

    lax.fori_loop(0, PEER_HEADS // ng, head_group, 0)


def _route(q, keys, tt=128):
    t = q.shape[0]
    spec = pl.BlockSpec((PEER_HEADS, N_KEYS, tt), lambda i: (0, 0, i))
    shape = (PEER_HEADS, N_KEYS, t)
    pspec = pl.BlockSpec((PEER_HEADS, N_KEYS // 2, tt), lambda i: (0, 0, i))
    pshape = (PEER_HEADS, N_KEYS // 2, t)
    return pl.pallas_call(
        _route_kernel,
        grid=(t // tt,),
        in_specs=[pl.BlockSpec((tt, q.shape[1]), lambda i: (i, 0)),
                  pl.BlockSpec(keys.shape, lambda i: (0, 0, 0))],
        out_specs=[spec, spec, pspec, pspec],
        out_shape=[jax.ShapeDtypeStruct(shape, F32), jax.ShapeDtypeStruct(shape, F32),
                   jax.ShapeDtypeStruct(pshape, jnp.uint32),
                   jax.ShapeDtypeStruct(pshape, jnp.uint32)],
        compiler_params=_params(("parallel",)),
    )(q, keys)


PEER_ROWS_PER_STEP = SUBLANES


PEER_SUB_TOKENS = 256
BF16_ROWS = 2 * SUBLANES


def _pair_bits(x):
    xb = x.astype(BF16)
    return lax.bitcast_convert_type(jnp.stack([xb, xb], axis=-1), jnp.uint32)


def _bf16_rows(pair_row, n):
    tile = pltpu.bitcast(jnp.broadcast_to(pair_row, (SUBLANES, pair_row.shape[1])), BF16)
    return jnp.concatenate([tile] * (n // BF16_ROWS), axis=0)


def _peer_kernel(xn_ref, down_ref, upt_ref, e1_ref, cnt1_ref, e2_ref, rank2_ref, h_ref, nw_ref,
                 o_ref, acc_ref, act_ref, ct_ref):
    j = pl.program_id(1)
    tt = xn_ref.shape[0]

    @pl.when(j == 0)
    def _():
        acc_ref[...] = jnp.zeros_like(acc_ref)

    nsub = tt // PEER_SUB_TOKENS

    pk = N_KEYS // 2

    def expert_acts(st):
        toks = slice(st * PEER_SUB_TOKENS, (st + 1) * PEER_SUB_TOKENS)
        act = _dot_nt(down_ref[...], xn_ref[toks, :])
        act_ref[st] = pltpu.bitcast(act.astype(BF16), jnp.uint32)

    expert_acts(0)
    for st in range(nsub):
        if st + 1 < nsub:
            expert_acts(st + 1)
        for ib in range(PEER_ROWS_PER_STEP):
            rows = slice(ib * pk, (ib + 1) * pk)
            for half in range(PEER_SUB_TOKENS // LANES):
                sub = slice(half * LANES, (half + 1) * LANES)
                lanes = slice(st * PEER_SUB_TOKENS + half * LANES,
                              st * PEER_SUB_TOKENS + (half + 1) * LANES)
                wsum = jnp.zeros((N_KEYS, LANES), BF16)
                for h in range(PEER_HEADS):
                    p = _bf16_rows(e1_ref[h, ib:ib + 1, lanes], N_KEYS)
                    c = _bf16_rows(cnt1_ref[h, ib:ib + 1, lanes], N_KEYS)
                    keys = slice(h * pk, (h + 1) * pk)
                    rank2 = pltpu.bitcast(rank2_ref[keys, lanes], BF16)
                    e2 = pltpu.bitcast(e2_ref[keys, lanes], BF16)
                    wsum = wsum + jnp.where(rank2 < c, e2, jnp.zeros((), BF16)) * p
                act = pltpu.bitcast(act_ref[st, rows, sub], BF16)
                ct_ref[st, rows, sub] = pltpu.bitcast(wsum * _gelu(act), jnp.uint32)
        toks = slice(st * PEER_SUB_TOKENS, (st + 1) * PEER_SUB_TOKENS)
        acc_ref[:, toks] += _dot(upt_ref[...], pltpu.bitcast(ct_ref[st], BF16))

    @pl.when(j == pl.num_programs(1) - 1)
    def _():
        h = h_ref[...] + acc_ref[...].T
        o_ref[...] = h * lax.rsqrt(jnp.mean(h * h, axis=-1, keepdims=True) + EPS) * nw_ref[...]


def _peer(xn2, down_bf16, upt_bf16, e1, cnt1, e2, rank2, h1, final_norm_w, tt=1024):
    t, d = xn2.shape
    tt = min(tt, t)
    eb = PEER_ROWS_PER_STEP * N_KEYS
    n_exp = down_bf16.shape[0]
    nsub = tt // PEER_SUB_TOKENS
    fac_i = pl.BlockSpec((PEER_HEADS, PEER_ROWS_PER_STEP, tt), lambda i, j: (0, j, i))
    fac_j = pl.BlockSpec((PEER_HEADS * N_KEYS // 2, tt), lambda i, j: (0, i))
    e2 = e2.reshape(PEER_HEADS * N_KEYS // 2, t)
    rank2 = rank2.reshape(PEER_HEADS * N_KEYS // 2, t)
    e1 = _pair_bits(e1)
    cnt1 = _pair_bits(cnt1)
    tile = pl.BlockSpec((tt, d), lambda i, j: (i, 0))
    return pl.pallas_call(
        _peer_kernel,
        grid=(t // tt, n_exp // eb),
        in_specs=[tile, pl.BlockSpec((eb, d), lambda i, j: (j, 0)),
                  pl.BlockSpec((d, eb), lambda i, j: (0, j)), fac_i, fac_i, fac_j, fac_j, tile,
                  pl.BlockSpec((1, d), lambda i, j: (0, 0))],
        out_specs=tile,
        out_shape=jax.ShapeDtypeStruct((t, d), F32),
        scratch_shapes=[pltpu.VMEM((d, tt), F32),
                        pltpu.VMEM((nsub, eb // 2, PEER_SUB_TOKENS), jnp.uint32),
                        pltpu.VMEM((nsub, eb // 2, PEER_SUB_TOKENS), jnp.uint32)],
        compiler_params=_params(("parallel", "arbitrary")),
    )(xn2, down_bf16, upt_bf16, e1, cnt1, e2, rank2, h1, final_norm_w)


def _layer(h2d, batch, seq, norm_mix_w, w_in, gate_bias, conv_dw_w, conv_dw_b, conv_ln_w,
           conv_ln_b, w_conv_out, dn_conv_w, dn_a_log, dn_dt_bias, dn_norm_w, w_dn_out, w_out,
           norm_ffn_w, peer_w_query, peer_sub_keys, peer_down, peer_up, out_norm_w):
    d = h2d.shape[1]
    conv_dim = conv_dw_w.shape[1]
    dn_dim = DN_HEADS * DN_HEAD_DIM
    o_qkv = 2 * conv_dim
    o_z = o_qkv + 3 * dn_dim
    o_a = o_z + dn_dim
    o_b = o_a + DN_HEADS
    o_gc = o_b + DN_HEADS
    w1 = jnp.concatenate([w_in[:, :o_a], w_in[:, o_gc:],
                          jnp.repeat(w_in[:, o_a:o_b], DN_HEAD_DIM, axis=1),
                          jnp.repeat(w_in[:, o_b:o_gc], DN_HEAD_DIM, axis=1)], axis=1).astype(BF16)
    proj = _inproj(h2d, norm_mix_w.reshape(1, d), w1)
    mc = _conv_branch(proj, batch, seq, conv_dw_w, conv_dw_b.reshape(1, -1),
                      conv_ln_w.reshape(1, -1), conv_ln_b.reshape(1, -1),
                      w_conv_out.astype(BF16), gate_bias[0].reshape(1, d))
    o_gated = _delta_branch(proj, batch, seq, dn_conv_w,
                            jnp.repeat(dn_a_log, DN_HEAD_DIM).reshape(1, dn_dim),
                            jnp.repeat(dn_dt_bias, DN_HEAD_DIM).reshape(1, dn_dim),
                            dn_norm_w.reshape(1, DN_HEAD_DIM))
    h1, xn2, q = _merge(h2d, mc, o_gated, proj, gate_bias[1].reshape(1, d),
                        w_dn_out.astype(BF16), w_out.astype(BF16), norm_ffn_w.reshape(1, d),
                        peer_w_query.astype(BF16))
    keys = peer_sub_keys.reshape(PEER_HEADS * 2, N_KEYS, PEER_HALF)
    e1, cnt1, e2, rank2 = _route(q, keys)
    return _peer(xn2, peer_down.astype(BF16), peer_up.T.astype(BF16), e1, cnt1, e2, rank2, h1,
                 out_norm_w.reshape(1, d))


def kernel(x, norm_mix_w, w_in, gate_bias, conv_dw_w, conv_dw_b, conv_ln_w, conv_ln_b,
           w_conv_out, dn_conv_w, dn_a_log, dn_dt_bias, dn_norm_w, w_dn_out, w_out, norm_ffn_w,
           peer_w_query, peer_sub_keys, peer_down, peer_up, final_norm_w):
    batch, seq, d = x.shape
    depth = w_in.shape[0]
    assert depth == 1, "the final RMSNorm is fused into the (single) layer's last stage"
    out = _layer(x.reshape(batch * seq, d), batch, seq, norm_mix_w[0], w_in[0], gate_bias[0],
                 conv_dw_w[0], conv_dw_b[0], conv_ln_w[0], conv_ln_b[0], w_conv_out[0],
                 dn_conv_w[0], dn_a_log[0], dn_dt_bias[0], dn_norm_w[0], w_dn_out[0], w_out[0],
                 norm_ffn_w[0], peer_w_query[0], peer_sub_keys[0], peer_down[0], peer_up[0],
                 final_norm_w)
    return out.reshape(batch, seq, d)
```

```python
import functools

import jax
import jax.numpy as jnp
from jax import lax
from jax.experimental import pallas as pl
from jax.experimental.pallas import tpu as pltpu

F32 = jnp.float32
BF16 = jnp.bfloat16
EPS = 1e-6
HIGHEST = lax.Precision.HIGHEST

CONV_WIDTH = 31
SHORT_CONV = 4
DN_HEADS = 8
DN_HEAD_DIM = 128
CHUNK = 64
N_KEYS = 128
PEER_HEADS = 8
PEER_HALF = 128
PEER_TOPK = 16

LANES = 128
SUBLANES = 8
VMEM_LIMIT = 56 * 1024 * 1024


def _params(semantics):
    return pltpu.CompilerParams(dimension_semantics=semantics, vmem_limit_bytes=VMEM_LIMIT)


def _dot(a, b, precision=None):
    return jnp.dot(a, b, preferred_element_type=F32, precision=precision)


def _dot_nt(a, b, precision=None):
    return lax.dot_general(a, b, (((1,), (1,)), ((), ())), preferred_element_type=F32,
                           precision=precision)


def _sigmoid(x):
    return jax.nn.sigmoid(x)


def _silu(x):
    return x * jax.nn.sigmoid(x)


def _gelu(x):
    return 0.5 * x * (1.0 + lax.erf(x * (2.0 ** -0.5)))


def _inproj_kernel(x_ref, nw_ref, w_ref, wab_ref, o_ref, oab_ref, xn_ref):
    @pl.when(pl.program_id(1) == 0)
    def _():
        x = x_ref[...]
        y = x * lax.rsqrt(jnp.mean(x * x, axis=-1, keepdims=True) + EPS)
        xn_ref[...] = (y * nw_ref[...]).astype(BF16)
        oab_ref[...] = _dot(xn_ref[...], wab_ref[...])

    o_ref[...] = _dot(xn_ref[...], w_ref[...]).astype(o_ref.dtype)


def _inproj(x2d, norm_w, w_bf16, wab_bf16, tm=1024, tn=512):
    t, d = x2d.shape
    tm = min(tm, t)
    n = w_bf16.shape[1]
    nab = wab_bf16.shape[1]
    return pl.pallas_call(
        _inproj_kernel,
        grid=(t // tm, n // tn),
        in_specs=[pl.BlockSpec((tm, d), lambda i, j: (i, 0)),
                  pl.BlockSpec((1, d), lambda i, j: (0, 0)),
                  pl.BlockSpec((d, tn), lambda i, j: (0, j)),
                  pl.BlockSpec((d, nab), lambda i, j: (0, 0))],
        out_specs=[pl.BlockSpec((tm, tn), lambda i, j: (i, j)),
                   pl.BlockSpec((tm, nab), lambda i, j: (i, 0))],
        out_shape=[jax.ShapeDtypeStruct((t, n), BF16), jax.ShapeDtypeStruct((t, nab), F32)],
        scratch_shapes=[pltpu.VMEM((tm, d), BF16)],
        compiler_params=_params(("parallel", "arbitrary")),
    )(x2d, norm_w, w_bf16, wab_bf16)


CONV_HALO = 32
CONV_ROWS = 128


def _conv_kernel(a_ref, b_ref, ha_ref, hb_ref, gc_ref, dww_ref, dwb_ref, lnw_ref, lnb_ref,
                 wco_ref, gb_ref, o_ref, buf_ref, cv_ref, hb16_ref, sh_ref):
    ts, c = a_ref.shape
    first = pl.program_id(1) == 0
    halo = ha_ref[...].astype(F32) * _sigmoid(hb_ref[...].astype(F32))
    buf_ref[0:CONV_HALO, :] = jnp.where(first, 0.0, halo)
    buf_ref[CONV_HALO:, :] = a_ref[...].astype(F32) * _sigmoid(b_ref[...].astype(F32))
    shift = CONV_HALO - (CONV_WIDTH - 1)

    def strip(s, carry):
        lane = pl.ds(pl.multiple_of(s * LANES, LANES), LANES)
        for rb in range(ts // CONV_ROWS):
            window = buf_ref[rb * CONV_ROWS:rb * CONV_ROWS + CONV_ROWS + CONV_HALO, lane]
            acc = jnp.zeros((CONV_ROWS, LANES), F32)
            for phase in range(SUBLANES):
                taps = [j for j in range(CONV_WIDTH) if (shift + j) % SUBLANES == phase]
                span = max((shift + j) // SUBLANES for j in taps) * SUBLANES + CONV_ROWS
                sh_ref[phase, 0:span, :] = window[phase:phase + span, :]
                for j in taps:
                    a = (shift + j) // SUBLANES * SUBLANES
                    acc = acc + dww_ref[j:j + 1, lane] * sh_ref[phase, a:a + CONV_ROWS, :]
            cv_ref[rb * CONV_ROWS:(rb + 1) * CONV_ROWS, lane] = acc + dwb_ref[:, lane]
        return carry

    lax.fori_loop(0, c // LANES, strip, 0)

    def ln_rows(r, carry):
        rows = pl.ds(pl.multiple_of(r * 64, 64), 64)
        v = cv_ref[rows, :]
        mu = jnp.mean(v, axis=-1, keepdims=True)
        vc = v - mu
        y = vc * lax.rsqrt(jnp.mean(vc * vc, axis=-1, keepdims=True) + EPS)
        y = y * lnw_ref[...] + lnb_ref[...]
        hb16_ref[rows, :] = _silu(y).astype(BF16)
        return carry

    lax.fori_loop(0, ts // 64, ln_rows, 0)
    y_conv = _dot(hb16_ref[...], wco_ref[...])
    o_ref[...] = _sigmoid(gc_ref[...].astype(F32) + gb_ref[...]) * y_conv


def _conv_branch(proj, batch, seq, conv_dw_w, conv_dw_b, conv_ln_w, conv_ln_b, w_conv_out_bf16,
                 gate_bias0, ts=512):
    t = proj.shape[0]
    c = conv_dw_w.shape[1]
    nt = seq // ts
    hpt = ts // CONV_HALO

    def cur(col):
        return pl.BlockSpec((ts, c), lambda b, i: (b * nt + i, col))

    def halo(col):
        return pl.BlockSpec((CONV_HALO, c),
                            lambda b, i: (jnp.maximum((b * nt + i) * hpt - 1, 0), col))

    def row(n):
        return pl.BlockSpec((n, c), lambda b, i: (0, 0))

    return pl.pallas_call(
        _conv_kernel,
        grid=(batch, nt),
        in_specs=[cur(0), cur(1), halo(0), halo(1), cur(6),
                  row(CONV_WIDTH), row(1), row(1), row(1),
                  pl.BlockSpec((c, c), lambda b, i: (0, 0)), row(1)],
        out_specs=pl.BlockSpec((ts, c), lambda b, i: (b * nt + i, 0)),
        out_shape=jax.ShapeDtypeStruct((t, c), F32),
        scratch_shapes=[pltpu.VMEM((ts + CONV_HALO, c), F32), pltpu.VMEM((ts, c), F32),
                        pltpu.VMEM((ts, c), BF16),
                        pltpu.VMEM((SUBLANES, CONV_ROWS + CONV_HALO, LANES), F32)],
        compiler_params=_params(("parallel", "arbitrary")),
    )(proj, proj, proj, proj, proj, conv_dw_w, conv_dw_b, conv_ln_w, conv_ln_b,
      w_conv_out_bf16, gate_bias0)


DN_HEADS_PER_STEP = 4
DN_CARRY = 8
DN_PRE_ROWS = 128
DN_CHUNKS_PER_ITER = 4


def _mm_bf(a, b):
    return jnp.dot(a.astype(BF16), b.astype(BF16), preferred_element_type=F32)


def _mm_nt_bf(a, b):
    return lax.dot_general(a.astype(BF16), b.astype(BF16), (((1,), (1,)), ((), ())),
                           preferred_element_type=F32)


def _tri_inverse_minus_eye(l_mats, blk16, off32, off64):
    n = range(len(l_mats))
    ld = [jnp.where(blk16, l_mats[u], 0.0) for u in n]
    p2 = [_mm_bf(ld[u], ld[u]) for u in n]
    t = [_mm_bf(ld[u], p2[u]) for u in n]
    p4 = [_mm_bf(p2[u], p2[u]) for u in n]
    x = [p2[u] - ld[u] - t[u] for u in n]
    t = [_mm_bf(x[u], p4[u]) for u in n]
    p8 = [_mm_bf(p4[u], p4[u]) for u in n]
    x = [x[u] + p4[u] + t[u] for u in n]
    t = [_mm_bf(x[u], p8[u]) for u in n]
    x = [x[u] + p8[u] + t[u] for u in n]
    for off in (off32, off64):
        c = [jnp.where(off, l_mats[u], 0.0) for u in n]
        y = [_mm_bf(x[u], c[u]) for u in n]
        y = [c[u] + y[u] for u in n]
        t = [_mm_bf(y[u], x[u]) for u in n]
        x = [x[u] - (y[u] + t[u]) for u in n]
    return x


def _delta_kernel(q_ref, k_ref, v_ref, z_ref, ab_ref, cwq_ref, cwk_ref, cwv_ref,
                  alog_ref, dtb_ref, nw_ref, o_ref,
                  qraw_ref, kraw_ref, vraw_ref, qs_ref, ks_ref, vs_ref, g_ref, beta_ref,
                  pq_ref, n_ref, o0_ref, gl_ref, state_ref, sh_ref):
    ts = q_ref.shape[0]
    nh = q_ref.shape[1] // DN_HEAD_DIM
    dk = DN_HEAD_DIM

    @pl.when(pl.program_id(2) == 0)
    def _():
        zeros = jnp.zeros((DN_CARRY, q_ref.shape[1]), F32)
        qraw_ref[0:DN_CARRY, :] = zeros
        kraw_ref[0:DN_CARRY, :] = zeros
        vraw_ref[0:DN_CARRY, :] = zeros
        state_ref[...] = jnp.zeros_like(state_ref)

    for raw_ref, src_ref, cw_ref, dst_ref, norm in (
            (qraw_ref, q_ref, cwq_ref, qs_ref, True),
            (kraw_ref, k_ref, cwk_ref, ks_ref, True),
            (vraw_ref, v_ref, cwv_ref, vs_ref, False)):
        raw_ref[DN_CARRY:, :] = src_ref[...].astype(F32)
        raw_all = raw_ref[...]
        for j in range(SHORT_CONV - 1):
            off = SHORT_CONV - 1 - j
            sh_ref[j, off:off + ts + DN_CARRY, :] = raw_all
        for rb in range(ts // DN_PRE_ROWS):
            blk = slice(DN_CARRY + rb * DN_PRE_ROWS, DN_CARRY + (rb + 1) * DN_PRE_ROWS)
            acc = cw_ref[SHORT_CONV - 1:SHORT_CONV, :] * raw_ref[blk, :]
            for j in range(SHORT_CONV - 1):
                acc = acc + cw_ref[j:j + 1, :] * sh_ref[j, blk, :]
            acc = _silu(acc)
            rows = slice(rb * DN_PRE_ROWS, (rb + 1) * DN_PRE_ROWS)
            if norm:
                for h in range(nh):
                    cols = slice(h * DN_HEAD_DIM, (h + 1) * DN_HEAD_DIM)
                    ah = acc[:, cols]
                    dst_ref[rows, cols] = ah * lax.rsqrt(
                        jnp.sum(ah * ah, axis=-1, keepdims=True) + EPS)
            else:
                dst_ref[rows, :] = acc
        raw_ref[0:DN_CARRY, :] = raw_ref[ts:ts + DN_CARRY, :]
    ab = ab_ref[...]
    g_cols = -jnp.exp(alog_ref[...]) * jax.nn.softplus(ab + dtb_ref[...])
    beta_cols = _sigmoid(ab)
    for h in range(nh):
        cols = slice(h * dk, (h + 1) * dk)
        g_ref[:, cols] = jnp.broadcast_to(g_cols[:, h:h + 1], (ts, dk))
        beta_ref[:, cols] = jnp.broadcast_to(beta_cols[:, nh + h:nh + h + 1], (ts, dk))

    rows_all = nh * CHUNK
    ri = lax.broadcasted_iota(jnp.int32, (rows_all, rows_all), 0)
    ci = lax.broadcasted_iota(jnp.int32, (rows_all, rows_all), 1)
    same_head = (ri // CHUNK) == (ci // CHUNK)
    causal = jnp.logical_and(same_head, ri >= ci)
    strict = jnp.logical_and(same_head, ri > ci)
    blk16 = (ri // 16) == (ci // 16)
    off32 = jnp.logical_and((ri // 32) == (ci // 32), (ri // 16) != (ci // 16))
    off64 = jnp.logical_and(same_head, (ri // 32) != (ci // 32))
    ti = lax.broadcasted_iota(jnp.int32, (CHUNK, CHUNK), 0)
    tj = lax.broadcasted_iota(jnp.int32, (CHUNK, CHUNK), 1)
    tri_incl = jnp.where(ti >= tj, 1.0, 0.0).astype(BF16)
    scale = dk ** -0.5
    units = range(DN_CHUNKS_PER_ITER)

    def stack_heads(x):
        return jnp.concatenate([x[:, h * dk:(h + 1) * dk] for h in range(nh)], axis=0)

    def local_body(it, carry):
        rows = [pl.ds(pl.multiple_of((it * DN_CHUNKS_PER_ITER + u) * CHUNK, CHUNK), CHUNK)
                for u in units]
        g_all = [g_ref[rows[u], :] for u in units]
        g_hi = [g_all[u].astype(BF16) for u in units]
        g_lo = [(g_all[u] - g_hi[u].astype(F32)).astype(BF16) for u in units]
        gcum_all = [jnp.dot(tri_incl, g_hi[u], preferred_element_type=F32)
                    + jnp.dot(tri_incl, g_lo[u], preferred_element_type=F32) for u in units]
        gcum = [stack_heads(gcum_all[u]) for u in units]
        g_last = [stack_heads(jnp.broadcast_to(gcum_all[u][CHUNK - 1:CHUNK, :],
                                               gcum_all[u].shape)) for u in units]
        qc = [stack_heads(qs_ref[rows[u], :]) * scale for u in units]
        kc = [stack_heads(ks_ref[rows[u], :]) for u in units]
        beta = [stack_heads(beta_ref[rows[u], :]) for u in units]
        g_row = [jnp.broadcast_to(gcum[u].T[0:1, :], (rows_all, rows_all)) for u in units]
        g_col = [jnp.concatenate([gcum[u]] * (rows_all // dk), axis=1) for u in units]
        decay = [jnp.where(causal, jnp.exp(jnp.where(causal, g_col[u] - g_row[u], 0.0)), 0.0)
                 for u in units]
        eg = [jnp.exp(gcum[u]) for u in units]
        kb = [kc[u] * beta[u] for u in units]
        rhs = [jnp.concatenate([stack_heads(vs_ref[rows[u], :]) * beta[u], kb[u] * eg[u]], axis=1)
               for u in units]
        kk = [_mm_nt_bf(kb[u], kc[u]) for u in units]
        qk = [_mm_nt_bf(qc[u], kc[u]) for u in units]
        l_mat = [jnp.where(strict, kk[u] * decay[u], 0.0) for u in units]
        x_inv = _tri_inverse_minus_eye(l_mat, blk16, off32, off64)
        t = [_mm_bf(x_inv[u], rhs[u]) for u in units]
        uw = [rhs[u] + t[u] for u in units]
        qk = [qk[u] * decay[u] for u in units]
        k_dec = [kc[u] * jnp.exp(g_last[u] - gcum[u]) for u in units]
        qk_uw = [_mm_bf(qk[u], uw[u]) for u in units]
        qe = [qc[u] * eg[u] - qk_uw[u][:, dk:] for u in units]
        for u in units:
            for h in range(nh):
                hr = slice(h * CHUNK, (h + 1) * CHUNK)
                slot = (it * DN_CHUNKS_PER_ITER + u) * nh + h
                kd_uw = _mm_bf(k_dec[u][hr, :].T, uw[u][hr, :])
                pq_ref[slot, 0:dk, :] = kd_uw[:, dk:].astype(BF16)
                pq_ref[slot, dk:, :] = qe[u][hr, :].astype(BF16)
                n_ref[slot] = kd_uw[:, :dk]
                o0_ref[slot] = qk_uw[u][hr, :dk]
                gl_ref[slot] = jnp.exp(g_last[u][h * CHUNK:h * CHUNK + SUBLANES, :])
        return carry

    lax.fori_loop(0, ts // (CHUNK * DN_CHUNKS_PER_ITER), local_body, 0)

    def chain_body(cidx, carry):
        rows = pl.ds(pl.multiple_of(cidx * CHUNK, CHUNK), CHUNK)
        for h in range(nh):
            cols = slice(h * dk, (h + 1) * dk)
            slot = cidx * nh + h
            state = state_ref[h]
            pq_s = jnp.dot(pq_ref[slot], state.astype(BF16), preferred_element_type=F32)
            state_ref[h] = state * gl_ref[slot][0:1, :] - pq_s[0:dk, :] + n_ref[slot]
            out = pq_s[dk:, :] + o0_ref[slot]
            y = out * lax.rsqrt(jnp.mean(out * out, axis=-1, keepdims=True) + EPS) * nw_ref[...]
            o_ref[rows, cols] = (y * _silu(z_ref[rows, cols].astype(F32))).astype(o_ref.dtype)
        return carry

    lax.fori_loop(0, ts // CHUNK, chain_body, 0)


def _delta_branch(proj, ab, batch, seq, dn_conv_w, alog_cols, dtb_cols, dn_norm_w, ts=512):
    t = proj.shape[0]
    hb = DN_HEADS_PER_STEP
    wcols = hb * DN_HEAD_DIM
    dn_dim = DN_HEADS * DN_HEAD_DIM
    nt = seq // ts
    per_group = dn_dim // wcols
    slots = (ts // CHUNK) * hb

    def col(group):
        return pl.BlockSpec((ts, wcols), lambda b, h, s: (b * nt + s, group * per_group + h))

    def cw(group):
        return pl.BlockSpec((SHORT_CONV, wcols), lambda b, h, s: (0, group * per_group + h))

    head_row = pl.BlockSpec((1, LANES), lambda b, h, s: (0, h))
    return pl.pallas_call(
        _delta_kernel,
        grid=(batch, DN_HEADS // hb, nt),
        in_specs=[col(2), col(3), col(4), col(5),
                  pl.BlockSpec((ts, LANES), lambda b, h, s: (b * nt + s, h)),
                  cw(0), cw(1), cw(2),
                  head_row, head_row, pl.BlockSpec((1, DN_HEAD_DIM), lambda b, h, s: (0, 0))],
        out_specs=pl.BlockSpec((ts, wcols), lambda b, h, s: (b * nt + s, h)),
        out_shape=jax.ShapeDtypeStruct((t, dn_dim), BF16),
        scratch_shapes=[pltpu.VMEM((ts + DN_CARRY, wcols), F32)] * 3
                       + [pltpu.VMEM((ts, wcols), F32)] * 5
                       + [pltpu.VMEM((slots, DN_HEAD_DIM + CHUNK, DN_HEAD_DIM), BF16),
                          pltpu.VMEM((slots, DN_HEAD_DIM, DN_HEAD_DIM), F32),
                          pltpu.VMEM((slots, CHUNK, DN_HEAD_DIM), F32),
                          pltpu.VMEM((slots, SUBLANES, DN_HEAD_DIM), F32),
                          pltpu.VMEM((hb, DN_HEAD_DIM, DN_HEAD_DIM), F32),
                          pltpu.VMEM((SHORT_CONV - 1, ts + 2 * DN_CARRY, wcols), F32)],
        compiler_params=_params(("parallel", "parallel", "arbitrary")),
    )(proj, proj, proj, proj, ab, dn_conv_w, dn_conv_w, dn_conv_w,
      alog_cols, dtb_cols, dn_norm_w)


def _merge_kernel(x_ref, mc_ref, o_ref, gd_ref, gb_ref, wdn_ref, wout_ref, nw_ref, wq_ref,
                  h_ref, xn_ref, q_ref):
    y_dn = _dot(o_ref[...], wdn_ref[...])
    merged = mc_ref[...] + _sigmoid(gd_ref[...].astype(F32) + gb_ref[...]) * y_dn
    h = x_ref[...] + _dot(merged.astype(BF16), wout_ref[...])
    h_ref[...] = h
    y = h * lax.rsqrt(jnp.mean(h * h, axis=-1, keepdims=True) + EPS) * nw_ref[...]
    xn = y.astype(BF16)
    xn_ref[...] = xn
    q_ref[...] = _dot(xn, wq_ref[...])


def _merge(x2d, mc, o_gated, proj, gate_bias1, w_dn_out_bf16, w_out_bf16, norm_ffn_w,
           w_query_bf16, tm=512):
    t, d = x2d.shape
    nq = w_query_bf16.shape[1]
    tile = pl.BlockSpec((tm, d), lambda i: (i, 0))
    row = pl.BlockSpec((1, d), lambda i: (0, 0))
    full = pl.BlockSpec((d, d), lambda i: (0, 0))
    return pl.pallas_call(
        _merge_kernel,
        grid=(t // tm,),
        in_specs=[tile, tile, tile, pl.BlockSpec((tm, d), lambda i: (i, 7)), row, full, full,
                  row, pl.BlockSpec((d, nq), lambda i: (0, 0))],
        out_specs=[tile, tile, pl.BlockSpec((tm, nq), lambda i: (i, 0))],
        out_shape=[jax.ShapeDtypeStruct((t, d), F32), jax.ShapeDtypeStruct((t, d), BF16),
                   jax.ShapeDtypeStruct((t, nq), F32)],
        compiler_params=_params(("parallel",)),
    )(x2d, mc, o_gated, proj, gate_bias1, w_dn_out_bf16, w_out_bf16, norm_ffn_w, w_query_bf16)


def _top16(scores, key):
    rank = jnp.full(scores.shape, float(PEER_TOPK), F32)
    tops = []
    work = scores
    for r in range(PEER_TOPK):
        m = jnp.max(work, axis=-2, keepdims=True)
        first = jnp.min(jnp.where(work == m, key, jnp.inf), axis=-2, keepdims=True)
        hit = key == first
        rank = jnp.where(hit, float(r), rank)
        work = jnp.where(hit, -jnp.inf, work)
        tops.append(m)
    return rank, tops


CAND_SPLIT = 8
CAND_ROWS = PEER_TOPK + (CAND_SPLIT - 1) * CAND_SPLIT + (PEER_TOPK - CAND_SPLIT)
ROUTE_HEADS_PER_ITER = 2


def _pair_bits(x):
    hi = pltpu.bitcast(x.astype(BF16).astype(F32), jnp.uint32)
    return jnp.bitwise_or(hi, lax.shift_right_logical(hi, jnp.uint32(16)))


def _route_kernel(q_ref, keys_ref, e1_ref, cnt1_ref, e2_ref, rank2_ref):
    tt = q_ref.shape[0]
    ng = ROUTE_HEADS_PER_ITER
    key12 = lax.broadcasted_iota(jnp.int32, (2 * ng, N_KEYS, tt), 1).astype(F32)
    r = lax.broadcasted_iota(jnp.int32, (ng, CAND_ROWS, tt), 1)
    mid = r - PEER_TOPK
    tail0 = PEER_TOPK + (CAND_SPLIT - 1) * CAND_SPLIT
    cand_key = jnp.where(
        r < PEER_TOPK, r,
        jnp.where(r < tail0,
                  (1 + jnp.right_shift(mid, 3)) * PEER_TOPK + jnp.bitwise_and(mid, CAND_SPLIT - 1),
                  (CAND_SPLIT + r - tail0) * PEER_TOPK)).astype(F32)

    def head_group(it, carry):
        def scores(hp):
            cols = pl.ds(pl.multiple_of(hp * PEER_HALF, PEER_HALF), PEER_HALF)
            return _dot_nt(keys_ref[hp], q_ref[:, cols], precision=HIGHEST)

        s12 = jnp.stack([scores(2 * ng * it + i) for i in range(2 * ng)])
        rank12, tops = _top16(s12, key12)
        top = [jnp.concatenate([t[i] for t in tops], axis=0) for i in range(2 * ng)]
        cands = []
        for g in range(ng):
            top1, top2 = top[2 * g], top[2 * g + 1]
            cands.append(jnp.concatenate(
                [top1[0:1, :] + top2]
                + [top1[k:k + 1, :] + top2[0:CAND_SPLIT, :] for k in range(1, CAND_SPLIT)]
                + [top1[CAND_SPLIT:, :] + top2[0:1, :]], axis=0))
        cand = jnp.stack(cands)
        crank, _ = _top16(cand, cand_key)
        for g in range(ng):
            h = ng * it + g
            top1, top2 = top[2 * g], top[2 * g + 1]
            sel = crank[g] < float(PEER_TOPK)
            z = jnp.sum(jnp.where(sel, jnp.exp(cands[g] - cands[g][0:1, :]), 0.0), axis=0,
                        keepdims=True)
            hits = jnp.where(sel, 1.0, 0.0)
            rank1 = rank12[2 * g]
            cnt1 = jnp.zeros((N_KEYS, tt), F32)
            for k in range(PEER_TOPK):
                if k == 0:
                    ck = jnp.sum(hits[0:PEER_TOPK, :], axis=0, keepdims=True)
                elif k < CAND_SPLIT:
                    lo = PEER_TOPK + (k - 1) * CAND_SPLIT
                    ck = jnp.sum(hits[lo:lo + CAND_SPLIT, :], axis=0, keepdims=True)
                else:
                    ck = hits[tail0 + k - CAND_SPLIT:tail0 + k - CAND_SPLIT + 1, :]
                cnt1 = jnp.where(rank1 == float(k), ck, cnt1)
            e1_ref[h] = _pair_bits(jnp.exp(s12[2 * g] - top1[0:1, :]))
            cnt1_ref[h] = _pair_bits(cnt1)
            e2_ref[h] = pltpu.bitcast(
                (jnp.exp(s12[2 * g + 1] - top2[0:1, :]) / z).astype(BF16), jnp.uint32)
            rank2_ref[h] = pltpu.bitcast(rank12[2 * g + 1].astype(BF16), jnp.uint32)
        return carry

    lax.fori_loop(0, PEER_HEADS // ng, head_group, 0)


def _route(q, keys, tt=128):
    t = q.shape[0]
    spec = pl.BlockSpec((PEER_HEADS, N_KEYS, tt), lambda i: (0, 0, i))
    shape = (PEER_HEADS, N_KEYS, t)
    pspec = pl.BlockSpec((PEER_HEADS, N_KEYS // 2, tt), lambda i: (0, 0, i))
    pshape = (PEER_HEADS, N_KEYS // 2, t)
    return pl.pallas_call(
        _route_kernel,
        grid=(t // tt,),
        in_specs=[pl.BlockSpec((tt, q.shape[1]), lambda i: (i, 0)),
                  pl.BlockSpec(keys.shape, lambda i: (0, 0, 0))],
        out_specs=[spec, spec, pspec, pspec],
        out_shape=[jax.ShapeDtypeStruct(shape, jnp.uint32), jax.ShapeDtypeStruct(shape, jnp.uint32),
                   jax.ShapeDtypeStruct(pshape, jnp.uint32),
                   jax.ShapeDtypeStruct(pshape, jnp.uint32)],
        compiler_params=_params(("parallel",)),
    )(q, keys)


PEER_ROWS_PER_STEP = SUBLANES


PEER_SUB_TOKENS = 256
BF16_ROWS = 2 * SUBLANES
PEER_ROWS_PER_LOAD = 2


def _bf16_rows(pair_row, n):
    tile = pltpu.bitcast(jnp.broadcast_to(pair_row, (SUBLANES, pair_row.shape[1])), BF16)
    return jnp.concatenate([tile] * (n // BF16_ROWS), axis=0)


def _peer_kernel(xn_ref, down_ref, down_next_ref, upt_ref, e1_ref, cnt1_ref, e2_ref, rank2_ref,
                 h_ref, nw_ref, o_ref, acc_ref, act_ref, ct_ref):
    j = pl.program_id(1)
    tt = xn_ref.shape[0]
    nsub = tt // PEER_SUB_TOKENS

    pk = N_KEYS // 2

    def expert_acts(st, rows_ref):
        toks = slice(st * PEER_SUB_TOKENS, (st + 1) * PEER_SUB_TOKENS)
        act = _dot_nt(rows_ref[...], xn_ref[toks, :])
        act_ref[st] = pltpu.bitcast(act.astype(BF16), jnp.uint32)

    @pl.when(j == 0)
    def _():
        acc_ref[...] = jnp.zeros_like(acc_ref)
        expert_acts(0, down_ref)

    for st in range(nsub):
        if st + 1 < nsub:
            expert_acts(st + 1, down_ref)
        else:
            expert_acts(0, down_next_ref)
        for ib0 in range(0, PEER_ROWS_PER_STEP, PEER_ROWS_PER_LOAD):
            group = range(ib0, ib0 + PEER_ROWS_PER_LOAD)
            for half in range(PEER_SUB_TOKENS // LANES):
                sub = slice(half * LANES, (half + 1) * LANES)
                lanes = slice(st * PEER_SUB_TOKENS + half * LANES,
                              st * PEER_SUB_TOKENS + (half + 1) * LANES)
                wsum = {ib: jnp.zeros((N_KEYS, LANES), BF16) for ib in group}
                for h in range(PEER_HEADS):
                    keys = slice(h * pk, (h + 1) * pk)
                    rank2 = pltpu.bitcast(rank2_ref[keys, lanes], BF16)
                    e2 = pltpu.bitcast(e2_ref[keys, lanes], BF16)
                    for ib in group:
                        p = _bf16_rows(e1_ref[h, ib:ib + 1, lanes], N_KEYS)
                        c = _bf16_rows(cnt1_ref[h, ib:ib + 1, lanes], N_KEYS)
                        wsum[ib] = wsum[ib] + jnp.where(rank2 < c, e2, jnp.zeros((), BF16)) * p
                for ib in group:
                    rows = slice(ib * pk, (ib + 1) * pk)
                    act = pltpu.bitcast(act_ref[st, rows, sub], BF16)
                    ct_ref[st, rows, sub] = pltpu.bitcast(wsum[ib] * _gelu(act), jnp.uint32)
        toks = slice(st * PEER_SUB_TOKENS, (st + 1) * PEER_SUB_TOKENS)
        acc_ref[:, toks] += _dot(upt_ref[...], pltpu.bitcast(ct_ref[st], BF16))

    @pl.when(j == pl.num_programs(1) - 1)
    def _():
        h = h_ref[...] + acc_ref[...].T
        o_ref[...] = h * lax.rsqrt(jnp.mean(h * h, axis=-1, keepdims=True) + EPS) * nw_ref[...]


def _peer(xn2, down_bf16, upt_bf16, e1, cnt1, e2, rank2, h1, final_norm_w, tt=1024):
    t, d = xn2.shape
    tt = min(tt, t)
    eb = PEER_ROWS_PER_STEP * N_KEYS
    n_exp = down_bf16.shape[0]
    nsub = tt // PEER_SUB_TOKENS
    fac_i = pl.BlockSpec((PEER_HEADS, PEER_ROWS_PER_STEP, tt), lambda i, j: (0, j, i))
    fac_j = pl.BlockSpec((PEER_HEADS * N_KEYS // 2, tt), lambda i, j: (0, i))
    e2 = e2.reshape(PEER_HEADS * N_KEYS // 2, t)
    rank2 = rank2.reshape(PEER_HEADS * N_KEYS // 2, t)
    tile = pl.BlockSpec((tt, d), lambda i, j: (i, 0))
    return pl.pallas_call(
        _peer_kernel,
        grid=(t // tt, n_exp // eb),
        in_specs=[tile, pl.BlockSpec((eb, d), lambda i, j: (j, 0)),
                  pl.BlockSpec((eb, d), lambda i, j: (jnp.minimum(j + 1, n_exp // eb - 1), 0)),
                  pl.BlockSpec((d, eb), lambda i, j: (0, j)), fac_i, fac_i, fac_j, fac_j, tile,
                  pl.BlockSpec((1, d), lambda i, j: (0, 0))],
        out_specs=tile,
        out_shape=jax.ShapeDtypeStruct((t, d), F32),
        scratch_shapes=[pltpu.VMEM((d, tt), F32),
                        pltpu.VMEM((nsub, eb // 2, PEER_SUB_TOKENS), jnp.uint32),
                        pltpu.VMEM((nsub, eb // 2, PEER_SUB_TOKENS), jnp.uint32)],
        compiler_params=_params(("parallel", "arbitrary")),
    )(xn2, down_bf16, down_bf16, upt_bf16, e1, cnt1, e2, rank2, h1, final_norm_w)


def _layer(h2d, batch, seq, norm_mix_w, w_in, gate_bias, conv_dw_w, conv_dw_b, conv_ln_w,
           conv_ln_b, w_conv_out, dn_conv_w, dn_a_log, dn_dt_bias, dn_norm_w, w_dn_out, w_out,
           norm_ffn_w, peer_w_query, peer_sub_keys, peer_down, peer_up, out_norm_w):
    d = h2d.shape[1]
    conv_dim = conv_dw_w.shape[1]
    dn_dim = DN_HEADS * DN_HEAD_DIM
    o_qkv = 2 * conv_dim
    o_z = o_qkv + 3 * dn_dim
    o_a = o_z + dn_dim
    o_b = o_a + DN_HEADS
    o_gc = o_b + DN_HEADS
    w1 = jnp.concatenate([w_in[:, :o_a], w_in[:, o_gc:]], axis=1).astype(BF16)
    hb = DN_HEADS_PER_STEP
    ngroups = DN_HEADS // hb

    def head_blocks(a_part, b_part):
        r = a_part.shape[0]
        blk = jnp.concatenate([a_part.reshape(r, ngroups, hb), b_part.reshape(r, ngroups, hb),
                               jnp.zeros((r, ngroups, LANES - 2 * hb), a_part.dtype)], axis=2)
        return blk.reshape(r, ngroups * LANES)

    wab = head_blocks(w_in[:, o_a:o_b], w_in[:, o_b:o_gc]).astype(BF16)
    zeros_h = jnp.zeros((1, DN_HEADS), F32)
    proj, ab = _inproj(h2d, norm_mix_w.reshape(1, d), w1, wab)
    mc = _conv_branch(proj, batch, seq, conv_dw_w, conv_dw_b.reshape(1, -1),
                      conv_ln_w.reshape(1, -1), conv_ln_b.reshape(1, -1),
                      w_conv_out.astype(BF16), gate_bias[0].reshape(1, d))
    o_gated = _delta_branch(proj, ab, batch, seq, dn_conv_w,
                            head_blocks(dn_a_log.reshape(1, DN_HEADS), zeros_h),
                            head_blocks(dn_dt_bias.reshape(1, DN_HEADS), zeros_h),
                            dn_norm_w.reshape(1, DN_HEAD_DIM))
    h1, xn2, q = _merge(h2d, mc, o_gated, proj, gate_bias[1].reshape(1, d),
                        w_dn_out.astype(BF16), w_out.astype(BF16), norm_ffn_w.reshape(1, d),
                        peer_w_query.astype(BF16))
    keys = peer_sub_keys.reshape(PEER_HEADS * 2, N_KEYS, PEER_HALF)
    e1, cnt1, e2, rank2 = _route(q, keys)
    return _peer(xn2, peer_down.astype(BF16), peer_up.T.astype(BF16), e1, cnt1, e2, rank2, h1,
                 out_norm_w.reshape(1, d))


def kernel(x, norm_mix_w, w_in, gate_bias, conv_dw_w, conv_dw_b, conv_ln_w, conv_ln_b,
           w_conv_out, dn_conv_w, dn_a_log, dn_dt_bias, dn_norm_w, w_dn_out, w_out, norm_ffn_w,
           peer_w_query, peer_sub_keys, peer_down, peer_up, final_norm_w):
    batch, seq, d = x.shape
    depth = w_in.shape[0]
    assert depth == 1, "the final RMSNorm is fused into the (single) layer's last stage"
    out = _layer(x.reshape(batch * seq, d), batch, seq, norm_mix_w[0], w_in[0], gate_bias[0],
                 conv_dw_w[0], conv_dw_b[0], conv_ln_w[0], conv_ln_b[0], w_conv_out[0],
                 dn_conv_w[0], dn_a_log[0], dn_dt_bias[0], dn_norm_w[0], w_dn_out[0], w_out[0],
                 norm_ffn_w[0], peer_w_query[0], peer_sub_keys[0], peer_down[0], peer_up[0],
                 final_norm_w)
    return out.reshape(batch, seq, d)
```

```python
import functools

import jax
import jax.numpy as jnp
from jax import lax
from jax.experimental import pallas as pl
from jax.experimental.pallas import tpu as pltpu

F32 = jnp.float32
BF16 = jnp.bfloat16
EPS = 1e-6
HIGHEST = lax.Precision.HIGHEST

CONV_WIDTH = 31
SHORT_CONV = 4
DN_HEADS = 8
DN_HEAD_DIM = 128
CHUNK = 64
N_KEYS = 128
PEER_HEADS = 8
PEER_HALF = 128
PEER_TOPK = 16

LANES = 128
SUBLANES = 8
VMEM_LIMIT = 56 * 1024 * 1024


def _params(semantics):
    return pltpu.CompilerParams(dimension_semantics=semantics, vmem_limit_bytes=VMEM_LIMIT)


def _dot(a, b, precision=None):
    return jnp.dot(a, b, preferred_element_type=F32, precision=precision)


def _dot_nt(a, b, precision=None):
    return lax.dot_general(a, b, (((1,), (1,)), ((), ())), preferred_element_type=F32,
                           precision=precision)


def _sigmoid(x):
    return jax.nn.sigmoid(x)


def _silu(x):
    return x * jax.nn.sigmoid(x)


def _gelu(x):
    return 0.5 * x * (1.0 + lax.erf(x * (2.0 ** -0.5)))


def _inproj_kernel(x_ref, nw_ref, w_ref, wab_ref, o_ref, oab_ref, xn_ref):
    @pl.when(pl.program_id(1) == 0)
    def _():
        x = x_ref[...]
        y = x * lax.rsqrt(jnp.mean(x * x, axis=-1, keepdims=True) + EPS)
        xn_ref[...] = (y * nw_ref[...]).astype(BF16)
        oab_ref[...] = _dot(xn_ref[...], wab_ref[...])

    o_ref[...] = _dot(xn_ref[...], w_ref[...]).astype(o_ref.dtype)


def _inproj(x2d, norm_w, w_bf16, wab_bf16, tm=1024, tn=1024):
    t, d = x2d.shape
    tm = min(tm, t)
    n = w_bf16.shape[1]
    nab = wab_bf16.shape[1]
    return pl.pallas_call(
        _inproj_kernel,
        grid=(t // tm, n // tn),
        in_specs=[pl.BlockSpec((tm, d), lambda i, j: (i, 0)),
                  pl.BlockSpec((1, d), lambda i, j: (0, 0)),
                  pl.BlockSpec((d, tn), lambda i, j: (0, j)),
                  pl.BlockSpec((d, nab), lambda i, j: (0, 0))],
        out_specs=[pl.BlockSpec((tm, tn), lambda i, j: (i, j)),
                   pl.BlockSpec((tm, nab), lambda i, j: (i, 0))],
        out_shape=[jax.ShapeDtypeStruct((t, n), BF16), jax.ShapeDtypeStruct((t, nab), F32)],
        scratch_shapes=[pltpu.VMEM((tm, d), BF16)],
        compiler_params=_params(("parallel", "arbitrary")),
    )(x2d, norm_w, w_bf16, wab_bf16)


CONV_HALO = 32
CONV_ROWS = 128


def _conv_kernel(a_ref, b_ref, ha_ref, hb_ref, gc_ref, dww_ref, dwb_ref, lnw_ref, lnb_ref,
                 wco_ref, gb_ref, o_ref, buf_ref, cv_ref, hb16_ref, sh_ref):
    ts, c = a_ref.shape
    first = pl.program_id(1) == 0
    halo = ha_ref[...].astype(F32) * _sigmoid(hb_ref[...].astype(F32))
    buf_ref[0:CONV_HALO, :] = jnp.where(first, 0.0, halo)
    buf_ref[CONV_HALO:, :] = a_ref[...].astype(F32) * _sigmoid(b_ref[...].astype(F32))
    shift = CONV_HALO - (CONV_WIDTH - 1)

    def strip(s, carry):
        lane = pl.ds(pl.multiple_of(s * LANES, LANES), LANES)
        for rb in range(ts // CONV_ROWS):
            window = buf_ref[rb * CONV_ROWS:rb * CONV_ROWS + CONV_ROWS + CONV_HALO, lane]
            acc = jnp.zeros((CONV_ROWS, LANES), F32)
            for phase in range(SUBLANES):
                taps = [j for j in range(CONV_WIDTH) if (shift + j) % SUBLANES == phase]
                span = max((shift + j) // SUBLANES for j in taps) * SUBLANES + CONV_ROWS
                sh_ref[phase, 0:span, :] = window[phase:phase + span, :]
                for j in taps:
                    a = (shift + j) // SUBLANES * SUBLANES
                    acc = acc + dww_ref[j:j + 1, lane] * sh_ref[phase, a:a + CONV_ROWS, :]
            cv_ref[rb * CONV_ROWS:(rb + 1) * CONV_ROWS, lane] = acc + dwb_ref[:, lane]
        return carry

    lax.fori_loop(0, c // LANES, strip, 0)

    def ln_rows(r, carry):
        rows = pl.ds(pl.multiple_of(r * 64, 64), 64)
        v = cv_ref[rows, :]
        mu = jnp.mean(v, axis=-1, keepdims=True)
        vc = v - mu
        y = vc * lax.rsqrt(jnp.mean(vc * vc, axis=-1, keepdims=True) + EPS)
        y = y * lnw_ref[...] + lnb_ref[...]
        hb16_ref[rows, :] = _silu(y).astype(BF16)
        return carry

    lax.fori_loop(0, ts // 64, ln_rows, 0)
    y_conv = _dot(hb16_ref[...], wco_ref[...])
    o_ref[...] = _sigmoid(gc_ref[...].astype(F32) + gb_ref[...]) * y_conv


def _conv_branch(proj, batch, seq, conv_dw_w, conv_dw_b, conv_ln_w, conv_ln_b, w_conv_out_bf16,
                 gate_bias0, ts=512):
    t = proj.shape[0]
    c = conv_dw_w.shape[1]
    nt = seq // ts
    hpt = ts // CONV_HALO

    def cur(col):
        return pl.BlockSpec((ts, c), lambda b, i: (b * nt + i, col))

    def halo(col):
        return pl.BlockSpec((CONV_HALO, c),
                            lambda b, i: (jnp.maximum((b * nt + i) * hpt - 1, 0), col))

    def row(n):
        return pl.BlockSpec((n, c), lambda b, i: (0, 0))

    return pl.pallas_call(
        _conv_kernel,
        grid=(batch, nt),
        in_specs=[cur(0), cur(1), halo(0), halo(1), cur(6),
                  row(CONV_WIDTH), row(1), row(1), row(1),
                  pl.BlockSpec((c, c), lambda b, i: (0, 0)), row(1)],
        out_specs=pl.BlockSpec((ts, c), lambda b, i: (b * nt + i, 0)),
        out_shape=jax.ShapeDtypeStruct((t, c), F32),
        scratch_shapes=[pltpu.VMEM((ts + CONV_HALO, c), F32), pltpu.VMEM((ts, c), F32),
                        pltpu.VMEM((ts, c), BF16),
                        pltpu.VMEM((SUBLANES, CONV_ROWS + CONV_HALO, LANES), F32)],
        compiler_params=_params(("parallel", "arbitrary")),
    )(proj, proj, proj, proj, proj, conv_dw_w, conv_dw_b, conv_ln_w, conv_ln_b,
      w_conv_out_bf16, gate_bias0)


DN_HEADS_PER_STEP = 4
DN_CARRY = 8
DN_PRE_ROWS = 128
DN_CHUNKS_PER_ITER = 4


def _mm_bf(a, b):
    return jnp.dot(a.astype(BF16), b.astype(BF16), preferred_element_type=F32)


def _mm_nt_bf(a, b):
    return lax.dot_general(a.astype(BF16), b.astype(BF16), (((1,), (1,)), ((), ())),
                           preferred_element_type=F32)


def _tri_inverse_minus_eye(l_mats, blk16, off32, off64):
    n = range(len(l_mats))
    ld = [jnp.where(blk16, l_mats[u], 0.0) for u in n]
    p2 = [_mm_bf(ld[u], ld[u]) for u in n]
    t = [_mm_bf(ld[u], p2[u]) for u in n]
    p4 = [_mm_bf(p2[u], p2[u]) for u in n]
    x = [p2[u] - ld[u] - t[u] for u in n]
    t = [_mm_bf(x[u], p4[u]) for u in n]
    p8 = [_mm_bf(p4[u], p4[u]) for u in n]
    x = [x[u] + p4[u] + t[u] for u in n]
    t = [_mm_bf(x[u], p8[u]) for u in n]
    x = [x[u] + p8[u] + t[u] for u in n]
    for off in (off32, off64):
        c = [jnp.where(off, l_mats[u], 0.0) for u in n]
        y = [_mm_bf(x[u], c[u]) for u in n]
        y = [c[u] + y[u] for u in n]
        t = [_mm_bf(y[u], x[u]) for u in n]
        x = [x[u] - (y[u] + t[u]) for u in n]
    return x


def _delta_kernel(q_ref, k_ref, v_ref, z_ref, ab_ref, cwq_ref, cwk_ref, cwv_ref,
                  alog_ref, dtb_ref, nw_ref, o_ref,
                  qraw_ref, kraw_ref, vraw_ref, qs_ref, ks_ref, vs_ref, g_ref, beta_ref,
                  pq_ref, n_ref, o0_ref, gl_ref, state_ref, sh_ref):
    ts = q_ref.shape[0]
    nh = q_ref.shape[1] // DN_HEAD_DIM
    dk = DN_HEAD_DIM

    @pl.when(pl.program_id(2) == 0)
    def _():
        zeros = jnp.zeros((DN_CARRY, q_ref.shape[1]), F32)
        qraw_ref[0:DN_CARRY, :] = zeros
        kraw_ref[0:DN_CARRY, :] = zeros
        vraw_ref[0:DN_CARRY, :] = zeros
        state_ref[...] = jnp.zeros_like(state_ref)

    for raw_ref, src_ref, cw_ref, dst_ref, norm in (
            (qraw_ref, q_ref, cwq_ref, qs_ref, True),
            (kraw_ref, k_ref, cwk_ref, ks_ref, True),
            (vraw_ref, v_ref, cwv_ref, vs_ref, False)):
        raw_ref[DN_CARRY:, :] = src_ref[...].astype(F32)
        raw_all = raw_ref[...]
        for j in range(SHORT_CONV - 1):
            off = SHORT_CONV - 1 - j
            sh_ref[j, off:off + ts + DN_CARRY, :] = raw_all
        for rb in range(ts // DN_PRE_ROWS):
            blk = slice(DN_CARRY + rb * DN_PRE_ROWS, DN_CARRY + (rb + 1) * DN_PRE_ROWS)
            acc = cw_ref[SHORT_CONV - 1:SHORT_CONV, :] * raw_ref[blk, :]
            for j in range(SHORT_CONV - 1):
                acc = acc + cw_ref[j:j + 1, :] * sh_ref[j, blk, :]
            acc = _silu(acc)
            rows = slice(rb * DN_PRE_ROWS, (rb + 1) * DN_PRE_ROWS)
            if norm:
                for h in range(nh):
                    cols = slice(h * DN_HEAD_DIM, (h + 1) * DN_HEAD_DIM)
                    ah = acc[:, cols]
                    dst_ref[rows, cols] = ah * lax.rsqrt(
                        jnp.sum(ah * ah, axis=-1, keepdims=True) + EPS)
            else:
                dst_ref[rows, :] = acc
        raw_ref[0:DN_CARRY, :] = raw_ref[ts:ts + DN_CARRY, :]
    ab = ab_ref[...]
    g_cols = -jnp.exp(alog_ref[...]) * jax.nn.softplus(ab + dtb_ref[...])
    beta_cols = _sigmoid(ab)
    for h in range(nh):
        cols = slice(h * dk, (h + 1) * dk)
        g_ref[:, cols] = jnp.broadcast_to(g_cols[:, h:h + 1], (ts, dk))
        beta_ref[:, cols] = jnp.broadcast_to(beta_cols[:, nh + h:nh + h + 1], (ts, dk))

    rows_all = nh * CHUNK
    ri = lax.broadcasted_iota(jnp.int32, (rows_all, rows_all), 0)
    ci = lax.broadcasted_iota(jnp.int32, (rows_all, rows_all), 1)
    same_head = (ri // CHUNK) == (ci // CHUNK)
    causal = jnp.logical_and(same_head, ri >= ci)
    strict = jnp.logical_and(same_head, ri > ci)
    blk16 = (ri // 16) == (ci // 16)
    off32 = jnp.logical_and((ri // 32) == (ci // 32), (ri // 16) != (ci // 16))
    off64 = jnp.logical_and(same_head, (ri // 32) != (ci // 32))
    ti = lax.broadcasted_iota(jnp.int32, (CHUNK, CHUNK), 0)
    tj = lax.broadcasted_iota(jnp.int32, (CHUNK, CHUNK), 1)
    tri_incl = jnp.where(ti >= tj, 1.0, 0.0).astype(BF16)
    scale = dk ** -0.5
    units = range(DN_CHUNKS_PER_ITER)

    def stack_heads(x):
        return jnp.concatenate([x[:, h * dk:(h + 1) * dk] for h in range(nh)], axis=0)

    def local_body(it, carry):
        rows = [pl.ds(pl.multiple_of((it * DN_CHUNKS_PER_ITER + u) * CHUNK, CHUNK), CHUNK)
                for u in units]
        g_all = [g_ref[rows[u], :] for u in units]
        g_hi = [g_all[u].astype(BF16) for u in units]
        g_lo = [(g_all[u] - g_hi[u].astype(F32)).astype(BF16) for u in units]
        gcum_all = [jnp.dot(tri_incl, g_hi[u], preferred_element_type=F32)
                    + jnp.dot(tri_incl, g_lo[u], preferred_element_type=F32) for u in units]
        gcum = [stack_heads(gcum_all[u]) for u in units]
        g_last = [stack_heads(jnp.broadcast_to(gcum_all[u][CHUNK - 1:CHUNK, :],
                                               gcum_all[u].shape)) for u in units]
        qc = [stack_heads(qs_ref[rows[u], :]) * scale for u in units]
        kc = [stack_heads(ks_ref[rows[u], :]) for u in units]
        beta = [stack_heads(beta_ref[rows[u], :]) for u in units]
        g_row = [jnp.broadcast_to(gcum[u].T[0:1, :], (rows_all, rows_all)) for u in units]
        g_col = [jnp.concatenate([gcum[u]] * (rows_all // dk), axis=1) for u in units]
        decay = [jnp.where(causal, jnp.exp(jnp.where(causal, g_col[u] - g_row[u], 0.0)), 0.0)
                 for u in units]
        eg = [jnp.exp(gcum[u]) for u in units]
        kb = [kc[u] * beta[u] for u in units]
        rhs = [jnp.concatenate([stack_heads(vs_ref[rows[u], :]) * beta[u], kb[u] * eg[u]], axis=1)
               for u in units]
        kk = [_mm_nt_bf(kb[u], kc[u]) for u in units]
        qk = [_mm_nt_bf(qc[u], kc[u]) for u in units]
        l_mat = [jnp.where(strict, kk[u] * decay[u], 0.0) for u in units]
        x_inv = _tri_inverse_minus_eye(l_mat, blk16, off32, off64)
        t = [_mm_bf(x_inv[u], rhs[u]) for u in units]
        uw = [rhs[u] + t[u] for u in units]
        qk = [qk[u] * decay[u] for u in units]
        k_dec = [kc[u] * jnp.exp(g_last[u] - gcum[u]) for u in units]
        qk_uw = [_mm_bf(qk[u], uw[u]) for u in units]
        qe = [qc[u] * eg[u] - qk_uw[u][:, dk:] for u in units]
        for u in units:
            for h in range(nh):
                hr = slice(h * CHUNK, (h + 1) * CHUNK)
                slot = (it * DN_CHUNKS_PER_ITER + u) * nh + h
                kd_uw = _mm_bf(k_dec[u][hr, :].T, uw[u][hr, :])
                pq_ref[slot, 0:dk, :] = kd_uw[:, dk:].astype(BF16)
                pq_ref[slot, dk:, :] = qe[u][hr, :].astype(BF16)
                n_ref[slot] = kd_uw[:, :dk]
                o0_ref[slot] = qk_uw[u][hr, :dk]
                gl_ref[slot] = jnp.exp(g_last[u][h * CHUNK:h * CHUNK + SUBLANES, :])
        return carry

    lax.fori_loop(0, ts // (CHUNK * DN_CHUNKS_PER_ITER), local_body, 0)

    def chain_body(cidx, carry):
        rows = pl.ds(pl.multiple_of(cidx * CHUNK, CHUNK), CHUNK)
        for h in range(nh):
            cols = slice(h * dk, (h + 1) * dk)
            slot = cidx * nh + h
            state = state_ref[h]
            pq_s = jnp.dot(pq_ref[slot], state.astype(BF16), preferred_element_type=F32)
            state_ref[h] = state * gl_ref[slot][0:1, :] - pq_s[0:dk, :] + n_ref[slot]
            out = pq_s[dk:, :] + o0_ref[slot]
            y = out * lax.rsqrt(jnp.mean(out * out, axis=-1, keepdims=True) + EPS) * nw_ref[...]
            o_ref[rows, cols] = (y * _silu(z_ref[rows, cols].astype(F32))).astype(o_ref.dtype)
        return carry

    lax.fori_loop(0, ts // CHUNK, chain_body, 0)


def _delta_branch(proj, ab, batch, seq, dn_conv_w, alog_cols, dtb_cols, dn_norm_w, ts=512):
    t = proj.shape[0]
    hb = DN_HEADS_PER_STEP
    wcols = hb * DN_HEAD_DIM
    dn_dim = DN_HEADS * DN_HEAD_DIM
    nt = seq // ts
    per_group = dn_dim // wcols
    slots = (ts // CHUNK) * hb

    def col(group):
        return pl.BlockSpec((ts, wcols), lambda b, h, s: (b * nt + s, group * per_group + h))

    def cw(group):
        return pl.BlockSpec((SHORT_CONV, wcols), lambda b, h, s: (0, group * per_group + h))

    head_row = pl.BlockSpec((1, LANES), lambda b, h, s: (0, h))
    return pl.pallas_call(
        _delta_kernel,
        grid=(batch, DN_HEADS // hb, nt),
        in_specs=[col(2), col(3), col(4), col(5),
                  pl.BlockSpec((ts, LANES), lambda b, h, s: (b * nt + s, h)),
                  cw(0), cw(1), cw(2),
                  head_row, head_row, pl.BlockSpec((1, DN_HEAD_DIM), lambda b, h, s: (0, 0))],
        out_specs=pl.BlockSpec((ts, wcols), lambda b, h, s: (b * nt + s, h)),
        out_shape=jax.ShapeDtypeStruct((t, dn_dim), BF16),
        scratch_shapes=[pltpu.VMEM((ts + DN_CARRY, wcols), F32)] * 3
                       + [pltpu.VMEM((ts, wcols), F32)] * 5
                       + [pltpu.VMEM((slots, DN_HEAD_DIM + CHUNK, DN_HEAD_DIM), BF16),
                          pltpu.VMEM((slots, DN_HEAD_DIM, DN_HEAD_DIM), F32),
                          pltpu.VMEM((slots, CHUNK, DN_HEAD_DIM), F32),
                          pltpu.VMEM((slots, SUBLANES, DN_HEAD_DIM), F32),
                          pltpu.VMEM((hb, DN_HEAD_DIM, DN_HEAD_DIM), F32),
                          pltpu.VMEM((SHORT_CONV - 1, ts + 2 * DN_CARRY, wcols), F32)],
        compiler_params=_params(("parallel", "parallel", "arbitrary")),
    )(proj, proj, proj, proj, ab, dn_conv_w, dn_conv_w, dn_conv_w,
      alog_cols, dtb_cols, dn_norm_w)


def _merge_kernel(x_ref, mc_ref, o_ref, gd_ref, gb_ref, wdn_ref, wout_ref, nw_ref, wq_ref,
                  h_ref, xn_ref, q_ref):
    y_dn = _dot(o_ref[...], wdn_ref[...])
    merged = mc_ref[...] + _sigmoid(gd_ref[...].astype(F32) + gb_ref[...]) * y_dn
    h = x_ref[...] + _dot(merged.astype(BF16), wout_ref[...])
    h_ref[...] = h
    y = h * lax.rsqrt(jnp.mean(h * h, axis=-1, keepdims=True) + EPS) * nw_ref[...]
    xn = y.astype(BF16)
    xn_ref[...] = xn
    q_ref[...] = _dot(xn, wq_ref[...])


def _merge(x2d, mc, o_gated, proj, gate_bias1, w_dn_out_bf16, w_out_bf16, norm_ffn_w,
           w_query_bf16, tm=512):
    t, d = x2d.shape
    nq = w_query_bf16.shape[1]
    tile = pl.BlockSpec((tm, d), lambda i: (i, 0))
    row = pl.BlockSpec((1, d), lambda i: (0, 0))
    full = pl.BlockSpec((d, d), lambda i: (0, 0))
    return pl.pallas_call(
        _merge_kernel,
        grid=(t // tm,),
        in_specs=[tile, tile, tile, pl.BlockSpec((tm, d), lambda i: (i, 7)), row, full, full,
                  row, pl.BlockSpec((d, nq), lambda i: (0, 0))],
        out_specs=[tile, tile, pl.BlockSpec((tm, nq), lambda i: (i, 0))],
        out_shape=[jax.ShapeDtypeStruct((t, d), F32), jax.ShapeDtypeStruct((t, d), BF16),
                   jax.ShapeDtypeStruct((t, nq), F32)],
        compiler_params=_params(("parallel",)),
    )(x2d, mc, o_gated, proj, gate_bias1, w_dn_out_bf16, w_out_bf16, norm_ffn_w, w_query_bf16)


def _top16(scores, key):
    rank = jnp.full(scores.shape, float(PEER_TOPK), F32)
    tops = []
    work = scores
    for r in range(PEER_TOPK):
        m = jnp.max(work, axis=-2, keepdims=True)
        if key is None:
            hit = work == m
        else:
            first = jnp.min(jnp.where(work == m, key, jnp.inf), axis=-2, keepdims=True)
            hit = key == first
        rank = jnp.where(hit, float(r), rank)
        work = jnp.where(hit, -jnp.inf, work)
        tops.append(m)
    taken = jnp.sum(jnp.where(rank < float(PEER_TOPK), 1.0, 0.0), axis=-2, keepdims=True)
    return rank, tops, taken == float(PEER_TOPK)


CAND_SPLIT = 8
CAND_ROWS = PEER_TOPK + (CAND_SPLIT - 1) * CAND_SPLIT + (PEER_TOPK - CAND_SPLIT)
ROUTE_HEADS_PER_ITER = 4


def _pair_bits(x):
    hi = pltpu.bitcast(x.astype(BF16).astype(F32), jnp.uint32)
    return jnp.bitwise_or(hi, lax.shift_right_logical(hi, jnp.uint32(16)))


def _route_kernel(q_ref, keys_ref, e1_ref, cnt1_ref, e2_ref, rank2_ref):
    tt = q_ref.shape[0]
    ng = ROUTE_HEADS_PER_ITER
    key12 = lax.broadcasted_iota(jnp.int32, (2 * ng, N_KEYS, tt), 1).astype(F32)
    r = lax.broadcasted_iota(jnp.int32, (ng, CAND_ROWS, tt), 1)
    mid = r - PEER_TOPK
    tail0 = PEER_TOPK + (CAND_SPLIT - 1) * CAND_SPLIT
    cand_key = jnp.where(
        r < PEER_TOPK, r,
        jnp.where(r < tail0,
                  (1 + jnp.right_shift(mid, 3)) * PEER_TOPK + jnp.bitwise_and(mid, CAND_SPLIT - 1),
                  (CAND_SPLIT + r - tail0) * PEER_TOPK)).astype(F32)

    def scores(hp):
        cols = pl.ds(pl.multiple_of(hp * PEER_HALF, PEER_HALF), PEER_HALF)
        return _dot_nt(keys_ref[hp], q_ref[:, cols], precision=HIGHEST)

    def route_group(it, exact):
        s12 = jnp.stack([scores(2 * ng * it + i) for i in range(2 * ng)])
        rank12, tops, clean12 = _top16(s12, key12 if exact else None)
        top = [jnp.concatenate([t[i] for t in tops], axis=0) for i in range(2 * ng)]
        cands = []
        for g in range(ng):
            top1, top2 = top[2 * g], top[2 * g + 1]
            cands.append(jnp.concatenate(
                [top1[0:1, :] + top2]
                + [top1[k:k + 1, :] + top2[0:CAND_SPLIT, :] for k in range(1, CAND_SPLIT)]
                + [top1[CAND_SPLIT:, :] + top2[0:1, :]], axis=0))
        cand = jnp.stack(cands)
        crank, _, cleanc = _top16(cand, cand_key if exact else None)
        for g in range(ng):
            h = ng * it + g
            top1, top2 = top[2 * g], top[2 * g + 1]
            sel = crank[g] < float(PEER_TOPK)
            z = jnp.sum(jnp.where(sel, jnp.exp(cands[g] - cands[g][0:1, :]), 0.0), axis=0,
                        keepdims=True)
            hits = jnp.where(sel, 1.0, 0.0)
            rank1 = rank12[2 * g]
            cnt1 = jnp.zeros((N_KEYS, tt), F32)
            for k in range(PEER_TOPK):
                if k == 0:
                    ck = jnp.sum(hits[0:PEER_TOPK, :], axis=0, keepdims=True)
                elif k < CAND_SPLIT:
                    lo = PEER_TOPK + (k - 1) * CAND_SPLIT
                    ck = jnp.sum(hits[lo:lo + CAND_SPLIT, :], axis=0, keepdims=True)
                else:
                    ck = hits[tail0 + k - CAND_SPLIT:tail0 + k - CAND_SPLIT + 1, :]
                cnt1 = jnp.where(rank1 == float(k), ck, cnt1)
            e1_ref[h] = _pair_bits(jnp.exp(s12[2 * g] - top1[0:1, :]))
            cnt1_ref[h] = _pair_bits(cnt1)
            e2_ref[h] = pltpu.bitcast(
                (jnp.exp(s12[2 * g + 1] - top2[0:1, :]) / z).astype(BF16), jnp.uint32)
            rank2_ref[h] = pltpu.bitcast(rank12[2 * g + 1].astype(BF16), jnp.uint32)
        dirty = (jnp.sum(jnp.where(clean12, 0.0, 1.0)) + jnp.sum(jnp.where(cleanc, 0.0, 1.0)))
        return dirty == 0.0

    def head_group(it, carry):
        ok = route_group(it, exact=False)

        @pl.when(jnp.logical_not(ok))
        def _():
            route_group(it, exact=True)

        return carry

    lax.fori_loop(0, PEER_HEADS // ng, head_group, 0)


def _route(q, keys, tt=128):
    t = q.shape[0]
    spec = pl.BlockSpec((PEER_HEADS, N_KEYS, tt), lambda i: (0, 0, i))
    shape = (PEER_HEADS, N_KEYS, t)
    pspec = pl.BlockSpec((PEER_HEADS, N_KEYS // 2, tt), lambda i: (0, 0, i))
    pshape = (PEER_HEADS, N_KEYS // 2, t)
    return pl.pallas_call(
        _route_kernel,
        grid=(t // tt,),
        in_specs=[pl.BlockSpec((tt, q.shape[1]), lambda i: (i, 0)),
                  pl.BlockSpec(keys.shape, lambda i: (0, 0, 0))],
        out_specs=[spec, spec, pspec, pspec],
        out_shape=[jax.ShapeDtypeStruct(shape, jnp.uint32), jax.ShapeDtypeStruct(shape, jnp.uint32),
                   jax.ShapeDtypeStruct(pshape, jnp.uint32),
                   jax.ShapeDtypeStruct(pshape, jnp.uint32)],
        compiler_params=_params(("parallel",)),
    )(q, keys)


PEER_ROWS_PER_STEP = SUBLANES


PEER_SUB_TOKENS = 256
BF16_ROWS = 2 * SUBLANES
PEER_ROWS_PER_LOAD = 2


def _bf16_rows(pair_row, n):
    tile = pltpu.bitcast(jnp.broadcast_to(pair_row, (SUBLANES, pair_row.shape[1])), BF16)
    return jnp.concatenate([tile] * (n // BF16_ROWS), axis=0)


def _peer_kernel(xn_ref, down_ref, down_next_ref, upt_ref, e1_ref, cnt1_ref, e2_ref, rank2_ref,
                 h_ref, nw_ref, o_ref, acc_ref, act_ref, ct_ref):
    j = pl.program_id(1)
    tt = xn_ref.shape[0]
    nsub = tt // PEER_SUB_TOKENS

    pk = N_KEYS // 2

    def expert_acts(st, rows_ref):
        toks = slice(st * PEER_SUB_TOKENS, (st + 1) * PEER_SUB_TOKENS)
        act = _dot_nt(rows_ref[...], xn_ref[toks, :])
        act_ref[st] = pltpu.bitcast(act.astype(BF16), jnp.uint32)

    @pl.when(j == 0)
    def _():
        acc_ref[...] = jnp.zeros_like(acc_ref)
        expert_acts(0, down_ref)

    for st in range(nsub):
        if st + 1 < nsub:
            expert_acts(st + 1, down_ref)
        else:
            expert_acts(0, down_next_ref)
        for ib0 in range(0, PEER_ROWS_PER_STEP, PEER_ROWS_PER_LOAD):
            group = range(ib0, ib0 + PEER_ROWS_PER_LOAD)
            for half in range(PEER_SUB_TOKENS // LANES):
                sub = slice(half * LANES, (half + 1) * LANES)
                lanes = slice(st * PEER_SUB_TOKENS + half * LANES,
                              st * PEER_SUB_TOKENS + (half + 1) * LANES)
                wsum = {ib: jnp.zeros((N_KEYS, LANES), BF16) for ib in group}
                for h in range(PEER_HEADS):
                    keys = slice(h * pk, (h + 1) * pk)
                    rank2 = pltpu.bitcast(rank2_ref[keys, lanes], BF16)
                    e2 = pltpu.bitcast(e2_ref[keys, lanes], BF16)
                    for ib in group:
                        p = _bf16_rows(e1_ref[h, ib:ib + 1, lanes], N_KEYS)
                        c = _bf16_rows(cnt1_ref[h, ib:ib + 1, lanes], N_KEYS)
                        wsum[ib] = wsum[ib] + jnp.where(rank2 < c, e2, jnp.zeros((), BF16)) * p
                for ib in group:
                    rows = slice(ib * pk, (ib + 1) * pk)
                    act = pltpu.bitcast(act_ref[st, rows, sub], BF16)
                    ct_ref[st, rows, sub] = pltpu.bitcast(wsum[ib] * _gelu(act), jnp.uint32)
        toks = slice(st * PEER_SUB_TOKENS, (st + 1) * PEER_SUB_TOKENS)
        acc_ref[:, toks] += _dot(upt_ref[...], pltpu.bitcast(ct_ref[st], BF16))

    @pl.when(j == pl.num_programs(1) - 1)
    def _():
        h = h_ref[...] + acc_ref[...].T
        o_ref[...] = h * lax.rsqrt(jnp.mean(h * h, axis=-1, keepdims=True) + EPS) * nw_ref[...]


def _peer(xn2, down_bf16, upt_bf16, e1, cnt1, e2, rank2, h1, final_norm_w, tt=1024):
    t, d = xn2.shape
    tt = min(tt, t)
    eb = PEER_ROWS_PER_STEP * N_KEYS
    n_exp = down_bf16.shape[0]
    nsub = tt // PEER_SUB_TOKENS
    fac_i = pl.BlockSpec((PEER_HEADS, PEER_ROWS_PER_STEP, tt), lambda i, j: (0, j, i))
    fac_j = pl.BlockSpec((PEER_HEADS * N_KEYS // 2, tt), lambda i, j: (0, i))
    e2 = e2.reshape(PEER_HEADS * N_KEYS // 2, t)
    rank2 = rank2.reshape(PEER_HEADS * N_KEYS // 2, t)
    tile = pl.BlockSpec((tt, d), lambda i, j: (i, 0))
    return pl.pallas_call(
        _peer_kernel,
        grid=(t // tt, n_exp // eb),
        in_specs=[tile, pl.BlockSpec((eb, d), lambda i, j: (j, 0)),
                  pl.BlockSpec((eb, d), lambda i, j: (jnp.minimum(j + 1, n_exp // eb - 1), 0)),
                  pl.BlockSpec((d, eb), lambda i, j: (0, j)), fac_i, fac_i, fac_j, fac_j, tile,
                  pl.BlockSpec((1, d), lambda i, j: (0, 0))],
        out_specs=tile,
        out_shape=jax.ShapeDtypeStruct((t, d), F32),
        scratch_shapes=[pltpu.VMEM((d, tt), F32),
                        pltpu.VMEM((nsub, eb // 2, PEER_SUB_TOKENS), jnp.uint32),
                        pltpu.VMEM((nsub, eb // 2, PEER_SUB_TOKENS), jnp.uint32)],
        compiler_params=_params(("parallel", "arbitrary")),
    )(xn2, down_bf16, down_bf16, upt_bf16, e1, cnt1, e2, rank2, h1, final_norm_w)


def _layer(h2d, batch, seq, norm_mix_w, w_in, gate_bias, conv_dw_w, conv_dw_b, conv_ln_w,
           conv_ln_b, w_conv_out, dn_conv_w, dn_a_log, dn_dt_bias, dn_norm_w, w_dn_out, w_out,
           norm_ffn_w, peer_w_query, peer_sub_keys, peer_down, peer_up, out_norm_w):
    d = h2d.shape[1]
    conv_dim = conv_dw_w.shape[1]
    dn_dim = DN_HEADS * DN_HEAD_DIM
    o_qkv = 2 * conv_dim
    o_z = o_qkv + 3 * dn_dim
    o_a = o_z + dn_dim
    o_b = o_a + DN_HEADS
    o_gc = o_b + DN_HEADS
    w1 = jnp.concatenate([w_in[:, :o_a], w_in[:, o_gc:]], axis=1).astype(BF16)
    hb = DN_HEADS_PER_STEP
    ngroups = DN_HEADS // hb

    def head_blocks(a_part, b_part):
        r = a_part.shape[0]
        blk = jnp.concatenate([a_part.reshape(r, ngroups, hb), b_part.reshape(r, ngroups, hb),
                               jnp.zeros((r, ngroups, LANES - 2 * hb), a_part.dtype)], axis=2)
        return blk.reshape(r, ngroups * LANES)

    wab = head_blocks(w_in[:, o_a:o_b], w_in[:, o_b:o_gc]).astype(BF16)
    zeros_h = jnp.zeros((1, DN_HEADS), F32)
    proj, ab = _inproj(h2d, norm_mix_w.reshape(1, d), w1, wab)
    mc = _conv_branch(proj, batch, seq, conv_dw_w, conv_dw_b.reshape(1, -1),
                      conv_ln_w.reshape(1, -1), conv_ln_b.reshape(1, -1),
                      w_conv_out.astype(BF16), gate_bias[0].reshape(1, d))
    o_gated = _delta_branch(proj, ab, batch, seq, dn_conv_w,
                            head_blocks(dn_a_log.reshape(1, DN_HEADS), zeros_h),
                            head_blocks(dn_dt_bias.reshape(1, DN_HEADS), zeros_h),
                            dn_norm_w.reshape(1, DN_HEAD_DIM))
    h1, xn2, q = _merge(h2d, mc, o_gated, proj, gate_bias[1].reshape(1, d),
                        w_dn_out.astype(BF16), w_out.astype(BF16), norm_ffn_w.reshape(1, d),
                        peer_w_query.astype(BF16))
    keys = peer_sub_keys.reshape(PEER_HEADS * 2, N_KEYS, PEER_HALF)
    e1, cnt1, e2, rank2 = _route(q, keys)
    return _peer(xn2, peer_down.astype(BF16), peer_up.T.astype(BF16), e1, cnt1, e2, rank2, h1,
                 out_norm_w.reshape(1, d))


def kernel(x, norm_mix_w, w_in, gate_bias, conv_dw_w, conv_dw_b, conv_ln_w, conv_ln_b,
           w_conv_out, dn_conv_w, dn_a_log, dn_dt_bias, dn_norm_w, w_dn_out, w_out, norm_ffn_w,
           peer_w_query, peer_sub_keys, peer_down, peer_up, final_norm_w):
    batch, seq, d = x.shape
    depth = w_in.shape[0]
    assert depth == 1, "the final RMSNorm is fused into the (single) layer's last stage"
    out = _layer(x.reshape(batch * seq, d), batch, seq, norm_mix_w[0], w_in[0], gate_bias[0],
                 conv_dw_w[0], conv_dw_b[0], conv_ln_w[0], conv_ln_b[0], w_conv_out[0],
                 dn_conv_w[0], dn_a_log[0], dn_dt_bias[0], dn_norm_w[0], w_dn_out[0], w_out[0],
                 norm_ffn_w[0], peer_w_query[0], peer_sub_keys[0], peer_down[0], peer_up[0],
                 final_norm_w)
    return out.reshape(batch, seq, d)
```

```python
import functools

import jax
import jax.numpy as jnp
from jax import lax
from jax.experimental import pallas as pl
from jax.experimental.pallas import tpu as pltpu

F32 = jnp.float32
BF16 = jnp.bfloat16
EPS = 1e-6
HIGHEST = lax.Precision.HIGHEST

CONV_WIDTH = 31
SHORT_CONV = 4
DN_HEADS = 8
DN_HEAD_DIM = 128
CHUNK = 64
N_KEYS = 128
PEER_HEADS = 8
PEER_HALF = 128
PEER_TOPK = 16

LANES = 128
SUBLANES = 8
VMEM_LIMIT = 56 * 1024 * 1024


def _params(semantics):
    return pltpu.CompilerParams(dimension_semantics=semantics, vmem_limit_bytes=VMEM_LIMIT)


def _dot(a, b, precision=None):
    return jnp.dot(a, b, preferred_element_type=F32, precision=precision)


def _dot_nt(a, b, precision=None):
    return lax.dot_general(a, b, (((1,), (1,)), ((), ())), preferred_element_type=F32,
                           precision=precision)


def _sigmoid(x):
    return jax.nn.sigmoid(x)


def _silu(x):
    return x * jax.nn.sigmoid(x)


def _gelu(x):
    return 0.5 * x * (1.0 + lax.erf(x * (2.0 ** -0.5)))


CONV_HALO = 32
CONV_ROWS = 128
LN_ROWS = 64
(COL_A, COL_B, COL_Q, COL_K, COL_V, COL_Z, COL_GC, COL_GD) = range(8)
N_STORED_COLS = 5


def _inproj_conv_kernel(x_ref, nw_ref, w_ref, wab_ref, dww_ref, dwb_ref, lnw_ref, lnb_ref,
                        wco_ref, gb_ref, o_ref, oab_ref, mc_ref,
                        xn_ref, buf_ref, cv_ref, hb16_ref, sh_ref, *, tiles_per_seq):
    i = pl.program_id(0)
    j = pl.program_id(1)
    tm, c = cv_ref.shape
    shift = CONV_HALO - (CONV_WIDTH - 1)

    def proj():
        return _dot(xn_ref[...], w_ref[...])

    def conv_strip(s):
        lane = slice(s * LANES, (s + 1) * LANES)
        for rb in range(tm // CONV_ROWS):
            window = buf_ref[rb * CONV_ROWS:rb * CONV_ROWS + CONV_ROWS + CONV_HALO, lane]
            acc = jnp.zeros((CONV_ROWS, LANES), F32)
            for phase in range(SUBLANES):
                taps = [k for k in range(CONV_WIDTH) if (shift + k) % SUBLANES == phase]
                span = max((shift + k) // SUBLANES for k in taps) * SUBLANES + CONV_ROWS
                sh_ref[phase, 0:span, :] = window[phase:phase + span, :]
                for k in taps:
                    a = (shift + k) // SUBLANES * SUBLANES
                    acc = acc + dww_ref[k:k + 1, lane] * sh_ref[phase, a:a + CONV_ROWS, :]
            cv_ref[rb * CONV_ROWS:(rb + 1) * CONV_ROWS, lane] = acc + dwb_ref[:, lane]

    def layer_norm_swish(lo, hi):
        for r in range(lo // LN_ROWS, hi // LN_ROWS):
            rows = slice(r * LN_ROWS, (r + 1) * LN_ROWS)
            v = cv_ref[rows, :]
            mu = jnp.mean(v, axis=-1, keepdims=True)
            vc = v - mu
            y = vc * lax.rsqrt(jnp.mean(vc * vc, axis=-1, keepdims=True) + EPS)
            hb16_ref[rows, :] = _silu(y * lnw_ref[...] + lnb_ref[...]).astype(BF16)

    @pl.when(j == COL_A)
    def _():
        x = x_ref[...]
        y = x * lax.rsqrt(jnp.mean(x * x, axis=-1, keepdims=True) + EPS)
        xn_ref[...] = (y * nw_ref[...]).astype(BF16)
        oab_ref[...] = _dot(xn_ref[...], wab_ref[...])
        prev = buf_ref[tm:tm + CONV_HALO, :]
        buf_ref[0:CONV_HALO, :] = jnp.where(i % tiles_per_seq == 0, 0.0, prev)
        buf_ref[CONV_HALO:, :] = proj()

    @pl.when(j == COL_B)
    def _():
        buf_ref[CONV_HALO:, :] = buf_ref[CONV_HALO:, :] * _sigmoid(proj())

    strips = c // LANES
    stored = (COL_Q, COL_K, COL_V, COL_Z)
    for n, col in enumerate(stored):
        @pl.when(j == col)
        def _(n=n):
            per_step = strips // len(stored)
            rows_per = tm // per_step
            for p in range(per_step):
                rows = slice(p * rows_per, (p + 1) * rows_per)
                o_ref[rows, :] = _dot(xn_ref[rows, :], w_ref[...]).astype(o_ref.dtype)
                conv_strip(n * per_step + p)

    @pl.when(j == COL_GC)
    def _():
        mc_ref[...] = _sigmoid(proj() + gb_ref[...])
        layer_norm_swish(0, tm // 2)

    @pl.when(j == COL_GD)
    def _():
        o_ref[...] = proj().astype(o_ref.dtype)
        layer_norm_swish(tm // 2, tm)
        mc_ref[...] = mc_ref[...] * _dot(hb16_ref[...], wco_ref[...])


def _inproj_conv(x2d, seq, norm_w, w_bf16, wab_bf16, conv_dw_w, conv_dw_b, conv_ln_w, conv_ln_b,
                 w_conv_out_bf16, gate_bias0, tm=1024):
    t, d = x2d.shape
    tm = min(tm, t, seq)
    tn = d
    ncols = w_bf16.shape[1] // tn
    nab = wab_bf16.shape[1]
    c = conv_dw_w.shape[1]
    assert ncols == COL_GD + 1 and c == tn and seq % tm == 0

    def row(n):
        return pl.BlockSpec((n, c), lambda i, j: (0, 0))

    def stored_col(i, j):
        return (i, jnp.clip(j - COL_Q, 0, N_STORED_COLS - 1))

    return pl.pallas_call(
        functools.partial(_inproj_conv_kernel, tiles_per_seq=seq // tm),
        grid=(t // tm, ncols),
        in_specs=[pl.BlockSpec((tm, d), lambda i, j: (i, 0)),
                  pl.BlockSpec((1, d), lambda i, j: (0, 0)),
                  pl.BlockSpec((d, tn), lambda i, j: (0, j)),
                  pl.BlockSpec((d, nab), lambda i, j: (0, 0)),
                  row(CONV_WIDTH), row(1), row(1), row(1),
                  pl.BlockSpec((c, c), lambda i, j: (0, 0)), row(1)],
        out_specs=[pl.BlockSpec((tm, tn), stored_col),
                   pl.BlockSpec((tm, nab), lambda i, j: (i, 0)),
                   pl.BlockSpec((tm, c), lambda i, j: (i, 0))],
        out_shape=[jax.ShapeDtypeStruct((t, N_STORED_COLS * tn), BF16),
                   jax.ShapeDtypeStruct((t, nab), F32),
                   jax.ShapeDtypeStruct((t, c), F32)],
        scratch_shapes=[pltpu.VMEM((tm, d), BF16),
                        pltpu.VMEM((tm + CONV_HALO, c), F32), pltpu.VMEM((tm, c), F32),
                        pltpu.VMEM((tm, c), BF16),
                        pltpu.VMEM((SUBLANES, CONV_ROWS + CONV_HALO, LANES), F32)],
        compiler_params=_params(("arbitrary", "arbitrary")),
    )(x2d, norm_w, w_bf16, wab_bf16, conv_dw_w, conv_dw_b, conv_ln_w, conv_ln_b,
      w_conv_out_bf16, gate_bias0)


DN_HEADS_PER_STEP = 4
DN_CARRY = 8
DN_PRE_ROWS = 128
DN_CHUNKS_PER_ITER = 4


def _mm_bf(a, b):
    return jnp.dot(a.astype(BF16), b.astype(BF16), preferred_element_type=F32)


def _mm_nt_bf(a, b):
    return lax.dot_general(a.astype(BF16), b.astype(BF16), (((1,), (1,)), ((), ())),
                           preferred_element_type=F32)


def _tri_inverse_minus_eye(l_mats, blk16, off32, off64):
    n = range(len(l_mats))
    ld = [jnp.where(blk16, l_mats[u], 0.0) for u in n]
    p2 = [_mm_bf(ld[u], ld[u]) for u in n]
    t = [_mm_bf(ld[u], p2[u]) for u in n]
    p4 = [_mm_bf(p2[u], p2[u]) for u in n]
    x = [p2[u] - ld[u] - t[u] for u in n]
    t = [_mm_bf(x[u], p4[u]) for u in n]
    p8 = [_mm_bf(p4[u], p4[u]) for u in n]
    x = [x[u] + p4[u] + t[u] for u in n]
    t = [_mm_bf(x[u], p8[u]) for u in n]
    x = [x[u] + p8[u] + t[u] for u in n]
    for off in (off32, off64):
        c = [jnp.where(off, l_mats[u], 0.0) for u in n]
        y = [_mm_bf(x[u], c[u]) for u in n]
        y = [c[u] + y[u] for u in n]
        t = [_mm_bf(y[u], x[u]) for u in n]
        x = [x[u] - (y[u] + t[u]) for u in n]
    return x


def _delta_kernel(q_ref, k_ref, v_ref, z_ref, ab_ref, cwq_ref, cwk_ref, cwv_ref,
                  alog_ref, dtb_ref, nw_ref, o_ref,
                  qraw_ref, kraw_ref, vraw_ref, qs_ref, ks_ref, vs_ref, g_ref, beta_ref,
                  pq_ref, n_ref, o0_ref, gl_ref, state_ref, sh_ref):
    ts = q_ref.shape[0]
    nh = q_ref.shape[1] // DN_HEAD_DIM
    dk = DN_HEAD_DIM

    @pl.when(pl.program_id(2) == 0)
    def _():
        zeros = jnp.zeros((DN_CARRY, q_ref.shape[1]), F32)
        qraw_ref[0:DN_CARRY, :] = zeros
        kraw_ref[0:DN_CARRY, :] = zeros
        vraw_ref[0:DN_CARRY, :] = zeros
        state_ref[...] = jnp.zeros_like(state_ref)

    for raw_ref, src_ref, cw_ref, dst_ref, norm in (
            (qraw_ref, q_ref, cwq_ref, qs_ref, True),
            (kraw_ref, k_ref, cwk_ref, ks_ref, True),
            (vraw_ref, v_ref, cwv_ref, vs_ref, False)):
        raw_ref[DN_CARRY:, :] = src_ref[...].astype(F32)
        raw_all = raw_ref[...]
        for j in range(SHORT_CONV - 1):
            off = SHORT_CONV - 1 - j
            sh_ref[j, off:off + ts + DN_CARRY, :] = raw_all
        for rb in range(ts // DN_PRE_ROWS):
            blk = slice(DN_CARRY + rb * DN_PRE_ROWS, DN_CARRY + (rb + 1) * DN_PRE_ROWS)
            acc = cw_ref[SHORT_CONV - 1:SHORT_CONV, :] * raw_ref[blk, :]
            for j in range(SHORT_CONV - 1):
                acc = acc + cw_ref[j:j + 1, :] * sh_ref[j, blk, :]
            acc = _silu(acc)
            rows = slice(rb * DN_PRE_ROWS, (rb + 1) * DN_PRE_ROWS)
            if norm:
                for h in range(nh):
                    cols = slice(h * DN_HEAD_DIM, (h + 1) * DN_HEAD_DIM)
                    ah = acc[:, cols]
                    dst_ref[rows, cols] = ah * lax.rsqrt(
                        jnp.sum(ah * ah, axis=-1, keepdims=True) + EPS)
            else:
                dst_ref[rows, :] = acc
        raw_ref[0:DN_CARRY, :] = raw_ref[ts:ts + DN_CARRY, :]
    ab = ab_ref[...]
    g_cols = -jnp.exp(alog_ref[...]) * jax.nn.softplus(ab + dtb_ref[...])
    beta_cols = _sigmoid(ab)
    for h in range(nh):
        cols = slice(h * dk, (h + 1) * dk)
        g_ref[:, cols] = jnp.broadcast_to(g_cols[:, h:h + 1], (ts, dk))
        beta_ref[:, cols] = jnp.broadcast_to(beta_cols[:, nh + h:nh + h + 1], (ts, dk))

    rows_all = nh * CHUNK
    ri = lax.broadcasted_iota(jnp.int32, (rows_all, rows_all), 0)
    ci = lax.broadcasted_iota(jnp.int32, (rows_all, rows_all), 1)
    same_head = (ri // CHUNK) == (ci // CHUNK)
    causal = jnp.logical_and(same_head, ri >= ci)
    strict = jnp.logical_and(same_head, ri > ci)
    blk16 = (ri // 16) == (ci // 16)
    off32 = jnp.logical_and((ri // 32) == (ci // 32), (ri // 16) != (ci // 16))
    off64 = jnp.logical_and(same_head, (ri // 32) != (ci // 32))
    ti = lax.broadcasted_iota(jnp.int32, (CHUNK, CHUNK), 0)
    tj = lax.broadcasted_iota(jnp.int32, (CHUNK, CHUNK), 1)
    tri_incl = jnp.where(ti >= tj, 1.0, 0.0).astype(BF16)
    scale = dk ** -0.5
    units = range(DN_CHUNKS_PER_ITER)

    def stack_heads(x):
        return jnp.concatenate([x[:, h * dk:(h + 1) * dk] for h in range(nh)], axis=0)

    def local_body(it, carry):
        rows = [pl.ds(pl.multiple_of((it * DN_CHUNKS_PER_ITER + u) * CHUNK, CHUNK), CHUNK)
                for u in units]
        g_all = [g_ref[rows[u], :] for u in units]
        g_hi = [g_all[u].astype(BF16) for u in units]
        g_lo = [(g_all[u] - g_hi[u].astype(F32)).astype(BF16) for u in units]
        gcum_all = [jnp.dot(tri_incl, g_hi[u], preferred_element_type=F32)
                    + jnp.dot(tri_incl, g_lo[u], preferred_element_type=F32) for u in units]
        gcum = [stack_heads(gcum_all[u]) for u in units]
        g_last = [stack_heads(jnp.broadcast_to(gcum_all[u][CHUNK - 1:CHUNK, :],
                                               gcum_all[u].shape)) for u in units]
        qc = [stack_heads(qs_ref[rows[u], :]) * scale for u in units]
        kc = [stack_heads(ks_ref[rows[u], :]) for u in units]
        beta = [stack_heads(beta_ref[rows[u], :]) for u in units]
        g_row = [jnp.broadcast_to(gcum[u].T[0:1, :], (rows_all, rows_all)) for u in units]
        g_col = [jnp.concatenate([gcum[u]] * (rows_all // dk), axis=1) for u in units]
        decay = [jnp.where(causal, jnp.exp(jnp.where(causal, g_col[u] - g_row[u], 0.0)), 0.0)
                 for u in units]
        eg = [jnp.exp(gcum[u]) for u in units]
        kb = [kc[u] * beta[u] for u in units]
        rhs = [jnp.concatenate([stack_heads(vs_ref[rows[u], :]) * beta[u], kb[u] * eg[u]], axis=1)
               for u in units]
        kk = [_mm_nt_bf(kb[u], kc[u]) for u in units]
        qk = [_mm_nt_bf(qc[u], kc[u]) for u in units]
        l_mat = [jnp.where(strict, kk[u] * decay[u], 0.0) for u in units]
        x_inv = _tri_inverse_minus_eye(l_mat, blk16, off32, off64)
        t = [_mm_bf(x_inv[u], rhs[u]) for u in units]
        uw = [rhs[u] + t[u] for u in units]
        qk = [qk[u] * decay[u] for u in units]
        k_dec = [kc[u] * jnp.exp(g_last[u] - gcum[u]) for u in units]
        qk_uw = [_mm_bf(qk[u], uw[u]) for u in units]
        qe = [qc[u] * eg[u] - qk_uw[u][:, dk:] for u in units]
        for u in units:
            for h in range(nh):
                hr = slice(h * CHUNK, (h + 1) * CHUNK)
                slot = (it * DN_CHUNKS_PER_ITER + u) * nh + h
                kd_uw = _mm_bf(k_dec[u][hr, :].T, uw[u][hr, :])
                pq_ref[slot, 0:dk, :] = kd_uw[:, dk:].astype(BF16)
                pq_ref[slot, dk:, :] = qe[u][hr, :].astype(BF16)
                n_ref[slot] = kd_uw[:, :dk]
                o0_ref[slot] = qk_uw[u][hr, :dk]
                gl_ref[slot] = jnp.exp(g_last[u][h * CHUNK:h * CHUNK + SUBLANES, :])
        return carry

    lax.fori_loop(0, ts // (CHUNK * DN_CHUNKS_PER_ITER), local_body, 0)

    def chain_body(cidx, carry):
        rows = pl.ds(pl.multiple_of(cidx * CHUNK, CHUNK), CHUNK)
        for h in range(nh):
            cols = slice(h * dk, (h + 1) * dk)
            slot = cidx * nh + h
            state = state_ref[h]
            pq_s = jnp.dot(pq_ref[slot], state.astype(BF16), preferred_element_type=F32)
            state_ref[h] = state * gl_ref[slot][0:1, :] - pq_s[0:dk, :] + n_ref[slot]
            out = pq_s[dk:, :] + o0_ref[slot]
            y = out * lax.rsqrt(jnp.mean(out * out, axis=-1, keepdims=True) + EPS) * nw_ref[...]
            o_ref[rows, cols] = (y * _silu(z_ref[rows, cols].astype(F32))).astype(o_ref.dtype)
        return carry

    lax.fori_loop(0, ts // CHUNK, chain_body, 0)


def _delta_branch(proj, ab, batch, seq, dn_conv_w, alog_cols, dtb_cols, dn_norm_w, ts=512):
    t = proj.shape[0]
    hb = DN_HEADS_PER_STEP
    wcols = hb * DN_HEAD_DIM
    dn_dim = DN_HEADS * DN_HEAD_DIM
    nt = seq // ts
    per_group = dn_dim // wcols
    slots = (ts // CHUNK) * hb

    def col(group):
        return pl.BlockSpec((ts, wcols), lambda b, h, s: (b * nt + s, group * per_group + h))

    def cw(group):
        return pl.BlockSpec((SHORT_CONV, wcols), lambda b, h, s: (0, group * per_group + h))

    head_row = pl.BlockSpec((1, LANES), lambda b, h, s: (0, h))
    return pl.pallas_call(
        _delta_kernel,
        grid=(batch, DN_HEADS // hb, nt),
        in_specs=[col(COL_Q - COL_Q), col(COL_K - COL_Q), col(COL_V - COL_Q), col(COL_Z - COL_Q),
                  pl.BlockSpec((ts, LANES), lambda b, h, s: (b * nt + s, h)),
                  cw(0), cw(1), cw(2),
                  head_row, head_row, pl.BlockSpec((1, DN_HEAD_DIM), lambda b, h, s: (0, 0))],
        out_specs=pl.BlockSpec((ts, wcols), lambda b, h, s: (b * nt + s, h)),
        out_shape=jax.ShapeDtypeStruct((t, dn_dim), BF16),
        scratch_shapes=[pltpu.VMEM((ts + DN_CARRY, wcols), F32)] * 3
                       + [pltpu.VMEM((ts, wcols), F32)] * 5
                       + [pltpu.VMEM((slots, DN_HEAD_DIM + CHUNK, DN_HEAD_DIM), BF16),
                          pltpu.VMEM((slots, DN_HEAD_DIM, DN_HEAD_DIM), F32),
                          pltpu.VMEM((slots, CHUNK, DN_HEAD_DIM), F32),
                          pltpu.VMEM((slots, SUBLANES, DN_HEAD_DIM), F32),
                          pltpu.VMEM((hb, DN_HEAD_DIM, DN_HEAD_DIM), F32),
                          pltpu.VMEM((SHORT_CONV - 1, ts + 2 * DN_CARRY, wcols), F32)],
        compiler_params=_params(("parallel", "parallel", "arbitrary")),
    )(proj, proj, proj, proj, ab, dn_conv_w, dn_conv_w, dn_conv_w,
      alog_cols, dtb_cols, dn_norm_w)


def _merge_kernel(x_ref, mc_ref, o_ref, gd_ref, gb_ref, wdn_ref, wout_ref, nw_ref, wq_ref,
                  h_ref, xn_ref, q_ref):
    y_dn = _dot(o_ref[...], wdn_ref[...])
    merged = mc_ref[...] + _sigmoid(gd_ref[...].astype(F32) + gb_ref[...]) * y_dn
    h = x_ref[...] + _dot(merged.astype(BF16), wout_ref[...])
    h_ref[...] = h
    y = h * lax.rsqrt(jnp.mean(h * h, axis=-1, keepdims=True) + EPS) * nw_ref[...]
    xn = y.astype(BF16)
    xn_ref[...] = xn
    q_ref[...] = _dot(xn, wq_ref[...])


def _merge(x2d, mc, o_gated, proj, gate_bias1, w_dn_out_bf16, w_out_bf16, norm_ffn_w,
           w_query_bf16, tm=512):
    t, d = x2d.shape
    nq = w_query_bf16.shape[1]
    tile = pl.BlockSpec((tm, d), lambda i: (i, 0))
    row = pl.BlockSpec((1, d), lambda i: (0, 0))
    full = pl.BlockSpec((d, d), lambda i: (0, 0))
    return pl.pallas_call(
        _merge_kernel,
        grid=(t // tm,),
        in_specs=[tile, tile, tile, pl.BlockSpec((tm, d), lambda i: (i, N_STORED_COLS - 1)),
                  row, full, full,
                  row, pl.BlockSpec((d, nq), lambda i: (0, 0))],
        out_specs=[tile, tile, pl.BlockSpec((tm, nq), lambda i: (i, 0))],
        out_shape=[jax.ShapeDtypeStruct((t, d), F32), jax.ShapeDtypeStruct((t, d), BF16),
                   jax.ShapeDtypeStruct((t, nq), F32)],
        compiler_params=_params(("parallel",)),
    )(x2d, mc, o_gated, proj, gate_bias1, w_dn_out_bf16, w_out_bf16, norm_ffn_w, w_query_bf16)


def _top16(scores, key):
    rank = jnp.full(scores.shape, float(PEER_TOPK), F32)
    tops = []
    work = scores
    for r in range(PEER_TOPK):
        m = jnp.max(work, axis=-2, keepdims=True)
        if key is None:
            hit = work == m
        else:
            first = jnp.min(jnp.where(work == m, key, jnp.inf), axis=-2, keepdims=True)
            hit = key == first
        rank = jnp.where(hit, float(r), rank)
        work = jnp.where(hit, -jnp.inf, work)
        tops.append(m)
    taken = jnp.sum(jnp.where(rank < float(PEER_TOPK), 1.0, 0.0), axis=-2, keepdims=True)
    return rank, tops, taken == float(PEER_TOPK)


CAND_SPLIT = 8
CAND_ROWS = PEER_TOPK + (CAND_SPLIT - 1) * CAND_SPLIT + (PEER_TOPK - CAND_SPLIT)
ROUTE_HEADS_PER_ITER = 4


def _route_kernel(q_ref, keys_ref, e1_ref, cnt1_ref, e2_ref, rank2_ref):
    tt = q_ref.shape[0]
    ng = ROUTE_HEADS_PER_ITER
    key12 = lax.broadcasted_iota(jnp.int32, (2 * ng, N_KEYS, tt), 1).astype(F32)
    r = lax.broadcasted_iota(jnp.int32, (ng, CAND_ROWS, tt), 1)
    mid = r - PEER_TOPK
    tail0 = PEER_TOPK + (CAND_SPLIT - 1) * CAND_SPLIT
    cand_key = jnp.where(
        r < PEER_TOPK, r,
        jnp.where(r < tail0,
                  (1 + jnp.right_shift(mid, 3)) * PEER_TOPK + jnp.bitwise_and(mid, CAND_SPLIT - 1),
                  (CAND_SPLIT + r - tail0) * PEER_TOPK)).astype(F32)

    def scores(hp):
        cols = pl.ds(pl.multiple_of(hp * PEER_HALF, PEER_HALF), PEER_HALF)
        return _dot_nt(keys_ref[hp], q_ref[:, cols], precision=HIGHEST)

    def route_group(it, exact):
        s12 = jnp.stack([scores(2 * ng * it + i) for i in range(2 * ng)])
        rank12, tops, clean12 = _top16(s12, key12 if exact else None)
        top = [jnp.concatenate([t[i] for t in tops], axis=0) for i in range(2 * ng)]
        cands = []
        for g in range(ng):
            top1, top2 = top[2 * g], top[2 * g + 1]
            cands.append(jnp.concatenate(
                [top1[0:1, :] + top2]
                + [top1[k:k + 1, :] + top2[0:CAND_SPLIT, :] for k in range(1, CAND_SPLIT)]
                + [top1[CAND_SPLIT:, :] + top2[0:1, :]], axis=0))
        cand = jnp.stack(cands)
        crank, _, cleanc = _top16(cand, cand_key if exact else None)
        for g in range(ng):
            h = ng * it + g
            top1, top2 = top[2 * g], top[2 * g + 1]
            sel = crank[g] < float(PEER_TOPK)
            z = jnp.sum(jnp.where(sel, jnp.exp(cands[g] - cands[g][0:1, :]), 0.0), axis=0,
                        keepdims=True)
            hits = jnp.where(sel, 1.0, 0.0)
            rank1 = rank12[2 * g]
            cnt1 = jnp.zeros((N_KEYS, tt), F32)
            for k in range(PEER_TOPK):
                if k == 0:
                    ck = jnp.sum(hits[0:PEER_TOPK, :], axis=0, keepdims=True)
                elif k < CAND_SPLIT:
                    lo = PEER_TOPK + (k - 1) * CAND_SPLIT
                    ck = jnp.sum(hits[lo:lo + CAND_SPLIT, :], axis=0, keepdims=True)
                else:
                    ck = hits[tail0 + k - CAND_SPLIT:tail0 + k - CAND_SPLIT + 1, :]
                cnt1 = jnp.where(rank1 == float(k), ck, cnt1)
            e1_ref[h] = jnp.exp(s12[2 * g] - top1[0:1, :])
            cnt1_ref[h] = cnt1
            e2_ref[h] = (jnp.exp(s12[2 * g + 1] - top2[0:1, :]) / z).astype(BF16)
            rank2_ref[h] = rank12[2 * g + 1].astype(BF16)
        dirty = (jnp.sum(jnp.where(clean12, 0.0, 1.0)) + jnp.sum(jnp.where(cleanc, 0.0, 1.0)))
        return dirty == 0.0

    def head_group(it, carry):
        ok = route_group(it, exact=False)

        @pl.when(jnp.logical_not(ok))
        def _():
            route_group(it, exact=True)

        return carry

    lax.fori_loop(0, PEER_HEADS // ng, head_group, 0)


def _route(q, keys, tt=128):
    t = q.shape[0]
    spec = pl.BlockSpec((PEER_HEADS, N_KEYS, tt), lambda i: (0, 0, i))
    shape = (PEER_HEADS, N_KEYS, t)
    return pl.pallas_call(
        _route_kernel,
        grid=(t // tt,),
        in_specs=[pl.BlockSpec((tt, q.shape[1]), lambda i: (i, 0)),
                  pl.BlockSpec(keys.shape, lambda i: (0, 0, 0))],
        out_specs=[spec] * 4,
        out_shape=[jax.ShapeDtypeStruct(shape, F32), jax.ShapeDtypeStruct(shape, F32),
                   jax.ShapeDtypeStruct(shape, BF16), jax.ShapeDtypeStruct(shape, BF16)],
        compiler_params=_params(("parallel",)),
    )(q, keys)


PEER_ROWS_PER_STEP = SUBLANES


PEER_SUB_TOKENS = 256
BF16_ROWS = 2 * SUBLANES
PEER_ROWS_PER_LOAD = 2


def _bf16_rows(tile, n):
    return jnp.concatenate([tile] * (n // BF16_ROWS), axis=0)


def _peer_kernel(xn_ref, down_ref, upt_ref, e1_ref, cnt1_ref, e2_ref, rank2_ref,
                 h_ref, nw_ref, o_ref, acc_ref, act_ref, ct_ref, r2s_ref, e2s_ref, e1s_ref,
                 cns_ref):
    j = pl.program_id(1)
    tt = xn_ref.shape[0]
    nsub = tt // PEER_SUB_TOKENS

    pk = N_KEYS

    def expert_acts(st, rows_ref):
        toks = slice(st * PEER_SUB_TOKENS, (st + 1) * PEER_SUB_TOKENS)
        act = _dot_nt(rows_ref[...], xn_ref[toks, :])
        act_ref[st] = act.astype(BF16)

    @pl.when(j == 0)
    def _():
        acc_ref[...] = jnp.zeros_like(acc_ref)
        r2s_ref[...] = rank2_ref[...]
        e2s_ref[...] = e2_ref[...]

    def stage_row_factors(st):
        toks = slice(st * PEER_SUB_TOKENS, (st + 1) * PEER_SUB_TOKENS)
        shape = (BF16_ROWS, PEER_SUB_TOKENS)
        for h in range(PEER_HEADS):
            for ib in range(PEER_ROWS_PER_STEP):
                e1s_ref[h, ib] = jnp.broadcast_to(e1_ref[h, ib:ib + 1, toks], shape).astype(BF16)
                cns_ref[h, ib] = jnp.broadcast_to(cnt1_ref[h, ib:ib + 1, toks], shape).astype(BF16)

    expert_acts(0, down_ref)
    for st in range(nsub):
        if st + 1 < nsub:
            expert_acts(st + 1, down_ref)
        stage_row_factors(st)
        for ib0 in range(0, PEER_ROWS_PER_STEP, PEER_ROWS_PER_LOAD):
            group = range(ib0, ib0 + PEER_ROWS_PER_LOAD)
            for half in range(PEER_SUB_TOKENS // LANES):
                sub = slice(half * LANES, (half + 1) * LANES)
                lanes = slice(st * PEER_SUB_TOKENS + half * LANES,
                              st * PEER_SUB_TOKENS + (half + 1) * LANES)
                wsum = {ib: jnp.zeros((N_KEYS, LANES), BF16) for ib in group}
                for h in range(PEER_HEADS):
                    keys = slice(h * pk, (h + 1) * pk)
                    rank2 = r2s_ref[keys, lanes]
                    e2 = e2s_ref[keys, lanes]
                    for ib in group:
                        p = _bf16_rows(e1s_ref[h, ib, :, sub], N_KEYS)
                        c = _bf16_rows(cns_ref[h, ib, :, sub], N_KEYS)
                        wsum[ib] = wsum[ib] + jnp.where(rank2 < c, e2, jnp.zeros((), BF16)) * p
                for ib in group:
                    rows = slice(ib * pk, (ib + 1) * pk)
                    ct_ref[st, rows, sub] = wsum[ib] * _gelu(act_ref[st, rows, sub])
        toks = slice(st * PEER_SUB_TOKENS, (st + 1) * PEER_SUB_TOKENS)
        acc_ref[:, toks] += _dot(upt_ref[...], ct_ref[st])

    @pl.when(j == pl.num_programs(1) - 1)
    def _():
        h = h_ref[...] + acc_ref[...].T
        o_ref[...] = h * lax.rsqrt(jnp.mean(h * h, axis=-1, keepdims=True) + EPS) * nw_ref[...]


def _peer(xn2, down_bf16, upt_bf16, e1, cnt1, e2, rank2, h1, final_norm_w, tt=1024):
    t, d = xn2.shape
    tt = min(tt, t)
    eb = PEER_ROWS_PER_STEP * N_KEYS
    n_exp = down_bf16.shape[0]
    nsub = tt // PEER_SUB_TOKENS
    fac_i = pl.BlockSpec((PEER_HEADS, PEER_ROWS_PER_STEP, tt), lambda i, j: (0, j, i))
    fac_j = pl.BlockSpec((PEER_HEADS * N_KEYS, tt), lambda i, j: (0, i))
    e2 = e2.reshape(PEER_HEADS * N_KEYS, t)
    rank2 = rank2.reshape(PEER_HEADS * N_KEYS, t)
    tile = pl.BlockSpec((tt, d), lambda i, j: (i, 0))
    return pl.pallas_call(
        _peer_kernel,
        grid=(t // tt, n_exp // eb),
        in_specs=[tile, pl.BlockSpec((eb, d), lambda i, j: (j, 0)),
                  pl.BlockSpec((d, eb), lambda i, j: (0, j)), fac_i, fac_i, fac_j, fac_j, tile,
                  pl.BlockSpec((1, d), lambda i, j: (0, 0))],
        out_specs=tile,
        out_shape=jax.ShapeDtypeStruct((t, d), F32),
        scratch_shapes=[pltpu.VMEM((d, tt), F32),
                        pltpu.VMEM((nsub, eb, PEER_SUB_TOKENS), BF16),
                        pltpu.VMEM((nsub, eb, PEER_SUB_TOKENS), BF16),
                        pltpu.VMEM((PEER_HEADS * N_KEYS, tt), BF16),
                        pltpu.VMEM((PEER_HEADS * N_KEYS, tt), BF16),
                        pltpu.VMEM((PEER_HEADS, PEER_ROWS_PER_STEP, BF16_ROWS, PEER_SUB_TOKENS), BF16),
                        pltpu.VMEM((PEER_HEADS, PEER_ROWS_PER_STEP, BF16_ROWS, PEER_SUB_TOKENS), BF16)],
        compiler_params=_params(("parallel", "arbitrary")),
    )(xn2, down_bf16, upt_bf16, e1, cnt1, e2, rank2, h1, final_norm_w)


def _layer(h2d, batch, seq, norm_mix_w, w_in, gate_bias, conv_dw_w, conv_dw_b, conv_ln_w,
           conv_ln_b, w_conv_out, dn_conv_w, dn_a_log, dn_dt_bias, dn_norm_w, w_dn_out, w_out,
           norm_ffn_w, peer_w_query, peer_sub_keys, peer_down, peer_up, out_norm_w):
    d = h2d.shape[1]
    conv_dim = conv_dw_w.shape[1]
    dn_dim = DN_HEADS * DN_HEAD_DIM
    o_qkv = 2 * conv_dim
    o_z = o_qkv + 3 * dn_dim
    o_a = o_z + dn_dim
    o_b = o_a + DN_HEADS
    o_gc = o_b + DN_HEADS
    w1 = jnp.concatenate([w_in[:, :o_a], w_in[:, o_gc:]], axis=1).astype(BF16)
    hb = DN_HEADS_PER_STEP
    ngroups = DN_HEADS // hb

    def head_blocks(a_part, b_part):
        r = a_part.shape[0]
        blk = jnp.concatenate([a_part.reshape(r, ngroups, hb), b_part.reshape(r, ngroups, hb),
                               jnp.zeros((r, ngroups, LANES - 2 * hb), a_part.dtype)], axis=2)
        return blk.reshape(r, ngroups * LANES)

    wab = head_blocks(w_in[:, o_a:o_b], w_in[:, o_b:o_gc]).astype(BF16)
    zeros_h = jnp.zeros((1, DN_HEADS), F32)
    proj, ab, mc = _inproj_conv(h2d, seq, norm_mix_w.reshape(1, d), w1, wab, conv_dw_w,
                                conv_dw_b.reshape(1, -1), conv_ln_w.reshape(1, -1),
                                conv_ln_b.reshape(1, -1), w_conv_out.astype(BF16),
                                gate_bias[0].reshape(1, d))
    o_gated = _delta_branch(proj, ab, batch, seq, dn_conv_w,
                            head_blocks(dn_a_log.reshape(1, DN_HEADS), zeros_h),
                            head_blocks(dn_dt_bias.reshape(1, DN_HEADS), zeros_h),
                            dn_norm_w.reshape(1, DN_HEAD_DIM))
    h1, xn2, q = _merge(h2d, mc, o_gated, proj, gate_bias[1].reshape(1, d),
                        w_dn_out.astype(BF16), w_out.astype(BF16), norm_ffn_w.reshape(1, d),
                        peer_w_query.astype(BF16))
    keys = peer_sub_keys.reshape(PEER_HEADS * 2, N_KEYS, PEER_HALF)
    e1, cnt1, e2, rank2 = _route(q, keys)
    return _peer(xn2, peer_down.astype(BF16), peer_up.T.astype(BF16), e1, cnt1, e2, rank2, h1,
                 out_norm_w.reshape(1, d))


def kernel(x, norm_mix_w, w_in, gate_bias, conv_dw_w, conv_dw_b, conv_ln_w, conv_ln_b,
           w_conv_out, dn_conv_w, dn_a_log, dn_dt_bias, dn_norm_w, w_dn_out, w_out, norm_ffn_w,
           peer_w_query, peer_sub_keys, peer_down, peer_up, final_norm_w):
    batch, seq, d = x.shape
    depth = w_in.shape[0]
    assert depth == 1, "the final RMSNorm is fused into the (single) layer's last stage"
    out = _layer(x.reshape(batch * seq, d), batch, seq, norm_mix_w[0], w_in[0], gate_bias[0],
                 conv_dw_w[0], conv_dw_b[0], conv_ln_w[0], conv_ln_b[0], w_conv_out[0],
                 dn_conv_w[0], dn_a_log[0], dn_dt_bias[0], dn_norm_w[0], w_dn_out[0], w_out[0],
                 norm_ffn_w[0], peer_w_query[0], peer_sub_keys[0], peer_down[0], peer_up[0],
                 final_norm_w)
    return out.reshape(batch, seq, d)
```

```python
import functools

import jax
import jax.numpy as jnp
from jax import lax
from jax.experimental import pallas as pl
from jax.experimental.pallas import tpu as pltpu

F32 = jnp.float32
BF16 = jnp.bfloat16
EPS = 1e-6
HIGHEST = lax.Precision.HIGHEST

CONV_WIDTH = 31
SHORT_CONV = 4
DN_HEADS = 8
DN_HEAD_DIM = 128
CHUNK = 64
N_KEYS = 128
PEER_HEADS = 8
PEER_HALF = 128
PEER_TOPK = 16

LANES = 128
SUBLANES = 8
VMEM_LIMIT = 56 * 1024 * 1024


def _params(semantics):
    return pltpu.CompilerParams(dimension_semantics=semantics, vmem_limit_bytes=VMEM_LIMIT)


def _dot(a, b, precision=None):
    return jnp.dot(a, b, preferred_element_type=F32, precision=precision)


def _dot_nt(a, b, precision=None):
    return lax.dot_general(a, b, (((1,), (1,)), ((), ())), preferred_element_type=F32,
                           precision=precision)


def _sigmoid(x):
    return jax.nn.sigmoid(x)


def _silu(x):
    return x * jax.nn.sigmoid(x)


def _gelu(x):
    return 0.5 * x * (1.0 + lax.erf(x * (2.0 ** -0.5)))


CONV_HALO = 32
CONV_ROWS = 128
LN_ROWS = 64
(COL_A, COL_B, COL_Q, COL_K, COL_V, COL_Z, COL_GC, COL_GD) = range(8)
N_STORED_COLS = 5


def _inproj_conv_kernel(x_ref, nw_ref, w_ref, wab_ref, dww_ref, dwb_ref, lnw_ref, lnb_ref,
                        wco_ref, gb_ref, o_ref, oab_ref, mc_ref,
                        xn_ref, buf_ref, cv_ref, hb16_ref, sh_ref, *, tiles_per_seq):
    i = pl.program_id(0)
    j = pl.program_id(1)
    tm, c = cv_ref.shape
    shift = CONV_HALO - (CONV_WIDTH - 1)

    def proj():
        return _dot(xn_ref[...], w_ref[...])

    def conv_strip(s):
        lane = slice(s * LANES, (s + 1) * LANES)
        for rb in range(tm // CONV_ROWS):
            window = buf_ref[rb * CONV_ROWS:rb * CONV_ROWS + CONV_ROWS + CONV_HALO, lane]
            acc = jnp.zeros((CONV_ROWS, LANES), F32)
            for phase in range(SUBLANES):
                taps = [k for k in range(CONV_WIDTH) if (shift + k) % SUBLANES == phase]
                span = max((shift + k) // SUBLANES for k in taps) * SUBLANES + CONV_ROWS
                sh_ref[phase, 0:span, :] = window[phase:phase + span, :]
                for k in taps:
                    a = (shift + k) // SUBLANES * SUBLANES
                    acc = acc + dww_ref[k:k + 1, lane] * sh_ref[phase, a:a + CONV_ROWS, :]
            cv_ref[rb * CONV_ROWS:(rb + 1) * CONV_ROWS, lane] = acc + dwb_ref[:, lane]

    def layer_norm_swish(lo, hi):
        for r in range(lo // LN_ROWS, hi // LN_ROWS):
            rows = slice(r * LN_ROWS, (r + 1) * LN_ROWS)
            v = cv_ref[rows, :]
            mu = jnp.mean(v, axis=-1, keepdims=True)
            vc = v - mu
            y = vc * lax.rsqrt(jnp.mean(vc * vc, axis=-1, keepdims=True) + EPS)
            hb16_ref[rows, :] = _silu(y * lnw_ref[...] + lnb_ref[...]).astype(BF16)

    @pl.when(j == COL_A)
    def _():
        x = x_ref[...]
        y = x * lax.rsqrt(jnp.mean(x * x, axis=-1, keepdims=True) + EPS)
        xn_ref[...] = (y * nw_ref[...]).astype(BF16)
        oab_ref[...] = _dot(xn_ref[...], wab_ref[...])
        prev = buf_ref[tm:tm + CONV_HALO, :]
        buf_ref[0:CONV_HALO, :] = jnp.where(i % tiles_per_seq == 0, 0.0, prev)
        buf_ref[CONV_HALO:, :] = proj()

    @pl.when(j == COL_B)
    def _():
        buf_ref[CONV_HALO:, :] = buf_ref[CONV_HALO:, :] * _sigmoid(proj())

    strips = c // LANES
    stored = (COL_Q, COL_K, COL_V, COL_Z)
    for n, col in enumerate(stored):
        @pl.when(j == col)
        def _(n=n):
            per_step = strips // len(stored)
            rows_per = tm // per_step
            for p in range(per_step):
                rows = slice(p * rows_per, (p + 1) * rows_per)
                o_ref[rows, :] = _dot(xn_ref[rows, :], w_ref[...]).astype(o_ref.dtype)
                conv_strip(n * per_step + p)

    @pl.when(j == COL_GC)
    def _():
        mc_ref[...] = _sigmoid(proj() + gb_ref[...])
        layer_norm_swish(0, tm // 2)

    @pl.when(j == COL_GD)
    def _():
        o_ref[...] = proj().astype(o_ref.dtype)
        layer_norm_swish(tm // 2, tm)
        mc_ref[...] = mc_ref[...] * _dot(hb16_ref[...], wco_ref[...])


def _inproj_conv(x2d, seq, norm_w, w_bf16, wab_bf16, conv_dw_w, conv_dw_b, conv_ln_w, conv_ln_b,
                 w_conv_out_bf16, gate_bias0, tm=1024):
    t, d = x2d.shape
    tm = min(tm, t, seq)
    tn = d
    ncols = w_bf16.shape[1] // tn
    nab = wab_bf16.shape[1]
    c = conv_dw_w.shape[1]
    assert ncols == COL_GD + 1 and c == tn and seq % tm == 0

    def row(n):
        return pl.BlockSpec((n, c), lambda i, j: (0, 0))

    def stored_col(i, j):
        return (i, jnp.clip(j - COL_Q, 0, N_STORED_COLS - 1))

    return pl.pallas_call(
        functools.partial(_inproj_conv_kernel, tiles_per_seq=seq // tm),
        grid=(t // tm, ncols),
        in_specs=[pl.BlockSpec((tm, d), lambda i, j: (i, 0)),
                  pl.BlockSpec((1, d), lambda i, j: (0, 0)),
                  pl.BlockSpec((d, tn), lambda i, j: (0, j)),
                  pl.BlockSpec((d, nab), lambda i, j: (0, 0)),
                  row(CONV_WIDTH), row(1), row(1), row(1),
                  pl.BlockSpec((c, c), lambda i, j: (0, 0)), row(1)],
        out_specs=[pl.BlockSpec((tm, tn), stored_col),
                   pl.BlockSpec((tm, nab), lambda i, j: (i, 0)),
                   pl.BlockSpec((tm, c), lambda i, j: (i, 0))],
        out_shape=[jax.ShapeDtypeStruct((t, N_STORED_COLS * tn), BF16),
                   jax.ShapeDtypeStruct((t, nab), F32),
                   jax.ShapeDtypeStruct((t, c), F32)],
        scratch_shapes=[pltpu.VMEM((tm, d), BF16),
                        pltpu.VMEM((tm + CONV_HALO, c), F32), pltpu.VMEM((tm, c), F32),
                        pltpu.VMEM((tm, c), BF16),
                        pltpu.VMEM((SUBLANES, CONV_ROWS + CONV_HALO, LANES), F32)],
        compiler_params=_params(("arbitrary", "arbitrary")),
    )(x2d, norm_w, w_bf16, wab_bf16, conv_dw_w, conv_dw_b, conv_ln_w, conv_ln_b,
      w_conv_out_bf16, gate_bias0)


DN_HEADS_PER_STEP = 4
DN_CARRY = 8
DN_PRE_ROWS = 128
DN_CHUNKS_PER_ITER = 4


def _mm_bf(a, b):
    return jnp.dot(a.astype(BF16), b.astype(BF16), preferred_element_type=F32)


def _mm_nt_bf(a, b):
    return lax.dot_general(a.astype(BF16), b.astype(BF16), (((1,), (1,)), ((), ())),
                           preferred_element_type=F32)


def _tri_inverse_minus_eye(l_mats, blk16, off32, off64):
    n = range(len(l_mats))
    ld = [jnp.where(blk16, l_mats[u], 0.0) for u in n]
    p2 = [_mm_bf(ld[u], ld[u]) for u in n]
    t = [_mm_bf(ld[u], p2[u]) for u in n]
    p4 = [_mm_bf(p2[u], p2[u]) for u in n]
    x = [p2[u] - ld[u] - t[u] for u in n]
    t = [_mm_bf(x[u], p4[u]) for u in n]
    p8 = [_mm_bf(p4[u], p4[u]) for u in n]
    x = [x[u] + p4[u] + t[u] for u in n]
    t = [_mm_bf(x[u], p8[u]) for u in n]
    x = [x[u] + p8[u] + t[u] for u in n]
    for off in (off32, off64):
        c = [jnp.where(off, l_mats[u], 0.0) for u in n]
        y = [_mm_bf(x[u], c[u]) for u in n]
        y = [c[u] + y[u] for u in n]
        t = [_mm_bf(y[u], x[u]) for u in n]
        x = [x[u] - (y[u] + t[u]) for u in n]
    return x


def _delta_kernel(q_ref, k_ref, v_ref, z_ref, ab_ref, cwq_ref, cwk_ref, cwv_ref,
                  alog_ref, dtb_ref, nw_ref, o_ref,
                  qraw_ref, kraw_ref, vraw_ref, qs_ref, ks_ref, vs_ref, g_ref, beta_ref,
                  pq_ref, n_ref, o0_ref, gl_ref, state_ref, sh_ref):
    ts = q_ref.shape[0]
    nh = q_ref.shape[1] // DN_HEAD_DIM
    dk = DN_HEAD_DIM

    @pl.when(pl.program_id(2) == 0)
    def _():
        zeros = jnp.zeros((DN_CARRY, q_ref.shape[1]), F32)
        qraw_ref[0:DN_CARRY, :] = zeros
        kraw_ref[0:DN_CARRY, :] = zeros
        vraw_ref[0:DN_CARRY, :] = zeros
        state_ref[...] = jnp.zeros_like(state_ref)

    for raw_ref, src_ref, cw_ref, dst_ref, norm in (
            (qraw_ref, q_ref, cwq_ref, qs_ref, True),
            (kraw_ref, k_ref, cwk_ref, ks_ref, True),
            (vraw_ref, v_ref, cwv_ref, vs_ref, False)):
        raw_ref[DN_CARRY:, :] = src_ref[...].astype(F32)
        raw_all = raw_ref[...]
        for j in range(SHORT_CONV - 1):
            off = SHORT_CONV - 1 - j
            sh_ref[j, off:off + ts + DN_CARRY, :] = raw_all
        for rb in range(ts // DN_PRE_ROWS):
            blk = slice(DN_CARRY + rb * DN_PRE_ROWS, DN_CARRY + (rb + 1) * DN_PRE_ROWS)
            acc = cw_ref[SHORT_CONV - 1:SHORT_CONV, :] * raw_ref[blk, :]
            for j in range(SHORT_CONV - 1):
                acc = acc + cw_ref[j:j + 1, :] * sh_ref[j, blk, :]
            acc = _silu(acc)
            rows = slice(rb * DN_PRE_ROWS, (rb + 1) * DN_PRE_ROWS)
            if norm:
                for h in range(nh):
                    cols = slice(h * DN_HEAD_DIM, (h + 1) * DN_HEAD_DIM)
                    ah = acc[:, cols]
                    dst_ref[rows, cols] = ah * lax.rsqrt(
                        jnp.sum(ah * ah, axis=-1, keepdims=True) + EPS)
            else:
                dst_ref[rows, :] = acc
        raw_ref[0:DN_CARRY, :] = raw_ref[ts:ts + DN_CARRY, :]
    ab = ab_ref[...]
    g_cols = -jnp.exp(alog_ref[...]) * jax.nn.softplus(ab + dtb_ref[...])
    beta_cols = _sigmoid(ab)
    for h in range(nh):
        cols = slice(h * dk, (h + 1) * dk)
        g_ref[:, cols] = jnp.broadcast_to(g_cols[:, h:h + 1], (ts, dk))
        beta_ref[:, cols] = jnp.broadcast_to(beta_cols[:, nh + h:nh + h + 1], (ts, dk))

    rows_all = nh * CHUNK
    ri = lax.broadcasted_iota(jnp.int32, (rows_all, rows_all), 0)
    ci = lax.broadcasted_iota(jnp.int32, (rows_all, rows_all), 1)
    same_head = (ri // CHUNK) == (ci // CHUNK)
    causal = jnp.logical_and(same_head, ri >= ci)
    strict = jnp.logical_and(same_head, ri > ci)
    blk16 = (ri // 16) == (ci // 16)
    off32 = jnp.logical_and((ri // 32) == (ci // 32), (ri // 16) != (ci // 16))
    off64 = jnp.logical_and(same_head, (ri // 32) != (ci // 32))
    ti = lax.broadcasted_iota(jnp.int32, (CHUNK, CHUNK), 0)
    tj = lax.broadcasted_iota(jnp.int32, (CHUNK, CHUNK), 1)
    tri_incl = jnp.where(ti >= tj, 1.0, 0.0).astype(BF16)
    scale = dk ** -0.5
    units = range(DN_CHUNKS_PER_ITER)

    def stack_heads(x):
        return jnp.concatenate([x[:, h * dk:(h + 1) * dk] for h in range(nh)], axis=0)

    def local_body(it, carry):
        rows = [pl.ds(pl.multiple_of((it * DN_CHUNKS_PER_ITER + u) * CHUNK, CHUNK), CHUNK)
                for u in units]
        g_all = [g_ref[rows[u], :] for u in units]
        g_hi = [g_all[u].astype(BF16) for u in units]
        g_lo = [(g_all[u] - g_hi[u].astype(F32)).astype(BF16) for u in units]
        gcum_all = [jnp.dot(tri_incl, g_hi[u], preferred_element_type=F32)
                    + jnp.dot(tri_incl, g_lo[u], preferred_element_type=F32) for u in units]
        gcum = [stack_heads(gcum_all[u]) for u in units]
        g_last = [stack_heads(jnp.broadcast_to(gcum_all[u][CHUNK - 1:CHUNK, :],
                                               gcum_all[u].shape)) for u in units]
        qc = [stack_heads(qs_ref[rows[u], :]) * scale for u in units]
        kc = [stack_heads(ks_ref[rows[u], :]) for u in units]
        beta = [stack_heads(beta_ref[rows[u], :]) for u in units]
        g_row = [jnp.broadcast_to(gcum[u].T[0:1, :], (rows_all, rows_all)) for u in units]
        g_col = [jnp.concatenate([gcum[u]] * (rows_all // dk), axis=1) for u in units]
        decay = [jnp.where(causal, jnp.exp(jnp.where(causal, g_col[u] - g_row[u], 0.0)), 0.0)
                 for u in units]
        eg = [jnp.exp(gcum[u]) for u in units]
        kb = [kc[u] * beta[u] for u in units]
        rhs = [jnp.concatenate([stack_heads(vs_ref[rows[u], :]) * beta[u], kb[u] * eg[u]], axis=1)
               for u in units]
        kk = [_mm_nt_bf(kb[u], kc[u]) for u in units]
        qk = [_mm_nt_bf(qc[u], kc[u]) for u in units]
        l_mat = [jnp.where(strict, kk[u] * decay[u], 0.0) for u in units]
        x_inv = _tri_inverse_minus_eye(l_mat, blk16, off32, off64)
        t = [_mm_bf(x_inv[u], rhs[u]) for u in units]
        uw = [rhs[u] + t[u] for u in units]
        qk = [qk[u] * decay[u] for u in units]
        k_dec = [kc[u] * jnp.exp(g_last[u] - gcum[u]) for u in units]
        qk_uw = [_mm_bf(qk[u], uw[u]) for u in units]
        qe = [qc[u] * eg[u] - qk_uw[u][:, dk:] for u in units]
        for u in units:
            for h in range(nh):
                hr = slice(h * CHUNK, (h + 1) * CHUNK)
                slot = (it * DN_CHUNKS_PER_ITER + u) * nh + h
                kd_uw = _mm_bf(k_dec[u][hr, :].T, uw[u][hr, :])
                pq_ref[slot, 0:dk, :] = kd_uw[:, dk:].astype(BF16)
                pq_ref[slot, dk:, :] = qe[u][hr, :].astype(BF16)
                n_ref[slot] = kd_uw[:, :dk]
                o0_ref[slot] = qk_uw[u][hr, :dk]
                gl_ref[slot] = jnp.exp(g_last[u][h * CHUNK:h * CHUNK + SUBLANES, :])
        return carry

    lax.fori_loop(0, ts // (CHUNK * DN_CHUNKS_PER_ITER), local_body, 0)

    def chain_body(cidx, carry):
        rows = pl.ds(pl.multiple_of(cidx * CHUNK, CHUNK), CHUNK)
        for h in range(nh):
            cols = slice(h * dk, (h + 1) * dk)
            slot = cidx * nh + h
            state = state_ref[h]
            pq_s = jnp.dot(pq_ref[slot], state.astype(BF16), preferred_element_type=F32)
            state_ref[h] = state * gl_ref[slot][0:1, :] - pq_s[0:dk, :] + n_ref[slot]
            out = pq_s[dk:, :] + o0_ref[slot]
            y = out * lax.rsqrt(jnp.mean(out * out, axis=-1, keepdims=True) + EPS) * nw_ref[...]
            o_ref[rows, cols] = (y * _silu(z_ref[rows, cols].astype(F32))).astype(o_ref.dtype)
        return carry

    lax.fori_loop(0, ts // CHUNK, chain_body, 0)


def _delta_branch(proj, ab, batch, seq, dn_conv_w, alog_cols, dtb_cols, dn_norm_w, ts=512):
    t = proj.shape[0]
    hb = DN_HEADS_PER_STEP
    wcols = hb * DN_HEAD_DIM
    dn_dim = DN_HEADS * DN_HEAD_DIM
    nt = seq // ts
    per_group = dn_dim // wcols
    slots = (ts // CHUNK) * hb

    def col(group):
        return pl.BlockSpec((ts, wcols), lambda b, h, s: (b * nt + s, group * per_group + h))

    def cw(group):
        return pl.BlockSpec((SHORT_CONV, wcols), lambda b, h, s: (0, group * per_group + h))

    head_row = pl.BlockSpec((1, LANES), lambda b, h, s: (0, h))
    return pl.pallas_call(
        _delta_kernel,
        grid=(batch, DN_HEADS // hb, nt),
        in_specs=[col(COL_Q - COL_Q), col(COL_K - COL_Q), col(COL_V - COL_Q), col(COL_Z - COL_Q),
                  pl.BlockSpec((ts, LANES), lambda b, h, s: (b * nt + s, h)),
                  cw(0), cw(1), cw(2),
                  head_row, head_row, pl.BlockSpec((1, DN_HEAD_DIM), lambda b, h, s: (0, 0))],
        out_specs=pl.BlockSpec((ts, wcols), lambda b, h, s: (b * nt + s, h)),
        out_shape=jax.ShapeDtypeStruct((t, dn_dim), BF16),
        scratch_shapes=[pltpu.VMEM((ts + DN_CARRY, wcols), F32)] * 3
                       + [pltpu.VMEM((ts, wcols), F32)] * 5
                       + [pltpu.VMEM((slots, DN_HEAD_DIM + CHUNK, DN_HEAD_DIM), BF16),
                          pltpu.VMEM((slots, DN_HEAD_DIM, DN_HEAD_DIM), F32),
                          pltpu.VMEM((slots, CHUNK, DN_HEAD_DIM), F32),
                          pltpu.VMEM((slots, SUBLANES, DN_HEAD_DIM), F32),
                          pltpu.VMEM((hb, DN_HEAD_DIM, DN_HEAD_DIM), F32),
                          pltpu.VMEM((SHORT_CONV - 1, ts + 2 * DN_CARRY, wcols), F32)],
        compiler_params=_params(("parallel", "parallel", "arbitrary")),
    )(proj, proj, proj, proj, ab, dn_conv_w, dn_conv_w, dn_conv_w,
      alog_cols, dtb_cols, dn_norm_w)


def _merge_kernel(x_ref, mc_ref, o_ref, gd_ref, gb_ref, wdn_ref, wout_ref, nw_ref, wq_ref,
                  h_ref, xn_ref, q_ref):
    y_dn = _dot(o_ref[...], wdn_ref[...])
    merged = mc_ref[...] + _sigmoid(gd_ref[...].astype(F32) + gb_ref[...]) * y_dn
    h = x_ref[...] + _dot(merged.astype(BF16), wout_ref[...])
    h_ref[...] = h
    y = h * lax.rsqrt(jnp.mean(h * h, axis=-1, keepdims=True) + EPS) * nw_ref[...]
    xn = y.astype(BF16)
    xn_ref[...] = xn
    q_ref[...] = _dot(xn, wq_ref[...])


def _merge(x2d, mc, o_gated, proj, gate_bias1, w_dn_out_bf16, w_out_bf16, norm_ffn_w,
           w_query_bf16, tm=512):
    t, d = x2d.shape
    nq = w_query_bf16.shape[1]
    tile = pl.BlockSpec((tm, d), lambda i: (i, 0))
    row = pl.BlockSpec((1, d), lambda i: (0, 0))
    full = pl.BlockSpec((d, d), lambda i: (0, 0))
    return pl.pallas_call(
        _merge_kernel,
        grid=(t // tm,),
        in_specs=[tile, tile, tile, pl.BlockSpec((tm, d), lambda i: (i, N_STORED_COLS - 1)),
                  row, full, full,
                  row, pl.BlockSpec((d, nq), lambda i: (0, 0))],
        out_specs=[tile, tile, pl.BlockSpec((tm, nq), lambda i: (i, 0))],
        out_shape=[jax.ShapeDtypeStruct((t, d), F32), jax.ShapeDtypeStruct((t, d), BF16),
                   jax.ShapeDtypeStruct((t, nq), F32)],
        compiler_params=_params(("parallel",)),
    )(x2d, mc, o_gated, proj, gate_bias1, w_dn_out_bf16, w_out_bf16, norm_ffn_w, w_query_bf16)


def _top16(scores, key):
    rank = jnp.full(scores.shape, float(PEER_TOPK), F32)
    tops = []
    work = scores
    for r in range(PEER_TOPK):
        m = jnp.max(work, axis=-2, keepdims=True)
        if key is None:
            hit = work == m
        else:
            first = jnp.min(jnp.where(work == m, key, jnp.inf), axis=-2, keepdims=True)
            hit = key == first
        rank = jnp.where(hit, float(r), rank)
        work = jnp.where(hit, -jnp.inf, work)
        tops.append(m)
    taken = jnp.sum(jnp.where(rank < float(PEER_TOPK), 1.0, 0.0), axis=-2, keepdims=True)
    return rank, tops, taken == float(PEER_TOPK)


CAND_SPLIT = 8
CAND_ROWS = PEER_TOPK + (CAND_SPLIT - 1) * CAND_SPLIT + (PEER_TOPK - CAND_SPLIT)
ROUTE_HEADS_PER_ITER = 4


def _route_kernel(q_ref, keys_ref, e1_ref, cnt1_ref, e2_ref, rank2_ref):
    tt = q_ref.shape[0]
    ng = ROUTE_HEADS_PER_ITER
    key12 = lax.broadcasted_iota(jnp.int32, (2 * ng, N_KEYS, tt), 1).astype(F32)
    r = lax.broadcasted_iota(jnp.int32, (ng, CAND_ROWS, tt), 1)
    mid = r - PEER_TOPK
    tail0 = PEER_TOPK + (CAND_SPLIT - 1) * CAND_SPLIT
    cand_key = jnp.where(
        r < PEER_TOPK, r,
        jnp.where(r < tail0,
                  (1 + jnp.right_shift(mid, 3)) * PEER_TOPK + jnp.bitwise_and(mid, CAND_SPLIT - 1),
                  (CAND_SPLIT + r - tail0) * PEER_TOPK)).astype(F32)

    def scores(hp):
        cols = pl.ds(pl.multiple_of(hp * PEER_HALF, PEER_HALF), PEER_HALF)
        return _dot_nt(keys_ref[hp], q_ref[:, cols], precision=HIGHEST)

    def route_group(it, exact):
        s12 = jnp.stack([scores(2 * ng * it + i) for i in range(2 * ng)])
        rank12, tops, clean12 = _top16(s12, key12 if exact else None)
        top = [jnp.concatenate([t[i] for t in tops], axis=0) for i in range(2 * ng)]
        cands = []
        for g in range(ng):
            top1, top2 = top[2 * g], top[2 * g + 1]
            cands.append(jnp.concatenate(
                [top1[0:1, :] + top2]
                + [top1[k:k + 1, :] + top2[0:CAND_SPLIT, :] for k in range(1, CAND_SPLIT)]
                + [top1[CAND_SPLIT:, :] + top2[0:1, :]], axis=0))
        cand = jnp.stack(cands)
        crank, _, cleanc = _top16(cand, cand_key if exact else None)
        for g in range(ng):
            h = ng * it + g
            top1, top2 = top[2 * g], top[2 * g + 1]
            sel = crank[g] < float(PEER_TOPK)
            z = jnp.sum(jnp.where(sel, jnp.exp(cands[g] - cands[g][0:1, :]), 0.0), axis=0,
                        keepdims=True)
            hits = jnp.where(sel, 1.0, 0.0)
            rank1 = rank12[2 * g]
            cnt1 = jnp.zeros((N_KEYS, tt), F32)
            for k in range(PEER_TOPK):
                if k == 0:
                    ck = jnp.sum(hits[0:PEER_TOPK, :], axis=0, keepdims=True)
                elif k < CAND_SPLIT:
                    lo = PEER_TOPK + (k - 1) * CAND_SPLIT
                    ck = jnp.sum(hits[lo:lo + CAND_SPLIT, :], axis=0, keepdims=True)
                else:
                    ck = hits[tail0 + k - CAND_SPLIT:tail0 + k - CAND_SPLIT + 1, :]
                cnt1 = jnp.where(rank1 == float(k), ck, cnt1)
            e1_ref[h] = jnp.exp(s12[2 * g] - top1[0:1, :])
            cnt1_ref[h] = cnt1
            e2_ref[h] = (jnp.exp(s12[2 * g + 1] - top2[0:1, :]) / z).astype(BF16)
            rank2_ref[h] = rank12[2 * g + 1].astype(BF16)
        dirty = (jnp.sum(jnp.where(clean12, 0.0, 1.0)) + jnp.sum(jnp.where(cleanc, 0.0, 1.0)))
        return dirty == 0.0

    def head_group(it, carry):
        ok = route_group(it, exact=False)

        @pl.when(jnp.logical_not(ok))
        def _():
            route_group(it, exact=True)

        return carry

    lax.fori_loop(0, PEER_HEADS // ng, head_group, 0)


def _route(q, keys, tt=128):
    t = q.shape[0]
    spec = pl.BlockSpec((PEER_HEADS, N_KEYS, tt), lambda i: (0, 0, i))
    shape = (PEER_HEADS, N_KEYS, t)
    return pl.pallas_call(
        _route_kernel,
        grid=(t // tt,),
        in_specs=[pl.BlockSpec((tt, q.shape[1]), lambda i: (i, 0)),
                  pl.BlockSpec(keys.shape, lambda i: (0, 0, 0))],
        out_specs=[spec] * 4,
        out_shape=[jax.ShapeDtypeStruct(shape, F32), jax.ShapeDtypeStruct(shape, F32),
                   jax.ShapeDtypeStruct(shape, BF16), jax.ShapeDtypeStruct(shape, BF16)],
        compiler_params=_params(("parallel",)),
    )(q, keys)


PEER_ROWS_PER_STEP = SUBLANES


PEER_SUB_TOKENS = 256
BF16_ROWS = 2 * SUBLANES
PEER_ROWS_PER_LOAD = 2


def _bf16_rows(tile, n):
    return jnp.concatenate([tile] * (n // BF16_ROWS), axis=0)


def _peer_kernel(xn_ref, down_ref, upt_ref, e1_ref, cnt1_ref, e2_ref, rank2_ref,
                 h_ref, nw_ref, o_ref, acc_ref, act_ref, ct_ref, r2s_ref, e2s_ref, e1s_ref,
                 cns_ref):
    j = pl.program_id(1)
    tt = xn_ref.shape[0]
    nsub = tt // PEER_SUB_TOKENS

    pk = N_KEYS

    def expert_acts(st, rows_ref):
        toks = slice(st * PEER_SUB_TOKENS, (st + 1) * PEER_SUB_TOKENS)
        act = _dot_nt(rows_ref[...], xn_ref[toks, :])
        act_ref[st] = act.astype(BF16)

    @pl.when(j == 0)
    def _():
        acc_ref[...] = jnp.zeros_like(acc_ref)
        r2s_ref[...] = rank2_ref[...]
        e2s_ref[...] = e2_ref[...]

    def stage_row_factors(st):
        toks = slice(st * PEER_SUB_TOKENS, (st + 1) * PEER_SUB_TOKENS)
        shape = (BF16_ROWS, PEER_SUB_TOKENS)
        for h in range(PEER_HEADS):
            for ib in range(PEER_ROWS_PER_STEP):
                e1s_ref[h, ib] = jnp.broadcast_to(e1_ref[h, ib:ib + 1, toks], shape).astype(BF16)
                cns_ref[h, ib] = jnp.broadcast_to(cnt1_ref[h, ib:ib + 1, toks], shape).astype(BF16)

    expert_acts(0, down_ref)
    for st in range(nsub):
        if st + 1 < nsub:
            expert_acts(st + 1, down_ref)
        stage_row_factors(st)
        for ib0 in range(0, PEER_ROWS_PER_STEP, PEER_ROWS_PER_LOAD):
            group = range(ib0, ib0 + PEER_ROWS_PER_LOAD)
            for half in range(PEER_SUB_TOKENS // LANES):
                sub = slice(half * LANES, (half + 1) * LANES)
                lanes = slice(st * PEER_SUB_TOKENS + half * LANES,
                              st * PEER_SUB_TOKENS + (half + 1) * LANES)
                wsum = {ib: jnp.zeros((N_KEYS, LANES), BF16) for ib in group}
                for h in range(PEER_HEADS):
                    keys = slice(h * pk, (h + 1) * pk)
                    rank2 = r2s_ref[keys, lanes]
                    e2 = e2s_ref[keys, lanes]
                    for ib in group:
                        p = _bf16_rows(e1s_ref[h, ib, :, sub], N_KEYS)
                        c = _bf16_rows(cns_ref[h, ib, :, sub], N_KEYS)
                        wsum[ib] = wsum[ib] + jnp.where(rank2 < c, e2, jnp.zeros((), BF16)) * p
                for ib in group:
                    rows = slice(ib * pk, (ib + 1) * pk)
                    ct_ref[st, rows, sub] = wsum[ib] * _gelu(act_ref[st, rows, sub])
        toks = slice(st * PEER_SUB_TOKENS, (st + 1) * PEER_SUB_TOKENS)
        acc_ref[:, toks] += _dot(upt_ref[...], ct_ref[st])

    @pl.when(j == pl.num_programs(1) - 1)
    def _():
        h = h_ref[...] + acc_ref[...].T
        o_ref[...] = h * lax.rsqrt(jnp.mean(h * h, axis=-1, keepdims=True) + EPS) * nw_ref[...]


def _peer(xn2, down_bf16, upt_bf16, e1, cnt1, e2, rank2, h1, final_norm_w, tt=512):
    t, d = xn2.shape
    tt = min(tt, t)
    eb = PEER_ROWS_PER_STEP * N_KEYS
    n_exp = down_bf16.shape[0]
    nsub = tt // PEER_SUB_TOKENS
    fac_i = pl.BlockSpec((PEER_HEADS, PEER_ROWS_PER_STEP, tt), lambda i, j: (0, j, i))
    fac_j = pl.BlockSpec((PEER_HEADS * N_KEYS, tt), lambda i, j: (0, i))
    e2 = e2.reshape(PEER_HEADS * N_KEYS, t)
    rank2 = rank2.reshape(PEER_HEADS * N_KEYS, t)
    tile = pl.BlockSpec((tt, d), lambda i, j: (i, 0))
    return pl.pallas_call(
        _peer_kernel,
        grid=(t // tt, n_exp // eb),
        in_specs=[tile, pl.BlockSpec((eb, d), lambda i, j: (j, 0)),
                  pl.BlockSpec((d, eb), lambda i, j: (0, j)), fac_i, fac_i, fac_j, fac_j, tile,
                  pl.BlockSpec((1, d), lambda i, j: (0, 0))],
        out_specs=tile,
        out_shape=jax.ShapeDtypeStruct((t, d), F32),
        scratch_shapes=[pltpu.VMEM((d, tt), F32),
                        pltpu.VMEM((nsub, eb, PEER_SUB_TOKENS), BF16),
                        pltpu.VMEM((nsub, eb, PEER_SUB_TOKENS), BF16),
                        pltpu.VMEM((PEER_HEADS * N_KEYS, tt), BF16),
                        pltpu.VMEM((PEER_HEADS * N_KEYS, tt), BF16),
                        pltpu.VMEM((PEER_HEADS, PEER_ROWS_PER_STEP, BF16_ROWS, PEER_SUB_TOKENS), BF16),
                        pltpu.VMEM((PEER_HEADS, PEER_ROWS_PER_STEP, BF16_ROWS, PEER_SUB_TOKENS), BF16)],
        compiler_params=_params(("parallel", "arbitrary")),
    )(xn2, down_bf16, upt_bf16, e1, cnt1, e2, rank2, h1, final_norm_w)


def _layer(h2d, batch, seq, norm_mix_w, w_in, gate_bias, conv_dw_w, conv_dw_b, conv_ln_w,
           conv_ln_b, w_conv_out, dn_conv_w, dn_a_log, dn_dt_bias, dn_norm_w, w_dn_out, w_out,
           norm_ffn_w, peer_w_query, peer_sub_keys, peer_down, peer_up, out_norm_w):
    d = h2d.shape[1]
    conv_dim = conv_dw_w.shape[1]
    dn_dim = DN_HEADS * DN_HEAD_DIM
    o_qkv = 2 * conv_dim
    o_z = o_qkv + 3 * dn_dim
    o_a = o_z + dn_dim
    o_b = o_a + DN_HEADS
    o_gc = o_b + DN_HEADS
    w1 = jnp.concatenate([w_in[:, :o_a], w_in[:, o_gc:]], axis=1).astype(BF16)
    hb = DN_HEADS_PER_STEP
    ngroups = DN_HEADS // hb

    def head_blocks(a_part, b_part):
        r = a_part.shape[0]
        blk = jnp.concatenate([a_part.reshape(r, ngroups, hb), b_part.reshape(r, ngroups, hb),
                               jnp.zeros((r, ngroups, LANES - 2 * hb), a_part.dtype)], axis=2)
        return blk.reshape(r, ngroups * LANES)

    wab = head_blocks(w_in[:, o_a:o_b], w_in[:, o_b:o_gc]).astype(BF16)
    zeros_h = jnp.zeros((1, DN_HEADS), F32)
    proj, ab, mc = _inproj_conv(h2d, seq, norm_mix_w.reshape(1, d), w1, wab, conv_dw_w,
                                conv_dw_b.reshape(1, -1), conv_ln_w.reshape(1, -1),
                                conv_ln_b.reshape(1, -1), w_conv_out.astype(BF16),
                                gate_bias[0].reshape(1, d))
    o_gated = _delta_branch(proj, ab, batch, seq, dn_conv_w,
                            head_blocks(dn_a_log.reshape(1, DN_HEADS), zeros_h),
                            head_blocks(dn_dt_bias.reshape(1, DN_HEADS), zeros_h),
                            dn_norm_w.reshape(1, DN_HEAD_DIM))
    h1, xn2, q = _merge(h2d, mc, o_gated, proj, gate_bias[1].reshape(1, d),
                        w_dn_out.astype(BF16), w_out.astype(BF16), norm_ffn_w.reshape(1, d),
                        peer_w_query.astype(BF16))
    keys = peer_sub_keys.reshape(PEER_HEADS * 2, N_KEYS, PEER_HALF)
    e1, cnt1, e2, rank2 = _route(q, keys)
    return _peer(xn2, peer_down.astype(BF16), peer_up.T.astype(BF16), e1, cnt1, e2, rank2, h1,
                 out_norm_w.reshape(1, d))


def kernel(x, norm_mix_w, w_in, gate_bias, conv_dw_w, conv_dw_b, conv_ln_w, conv_ln_b,
           w_conv_out, dn_conv_w, dn_a_log, dn_dt_bias, dn_norm_w, w_dn_out, w_out, norm_ffn_w,
           peer_w_query, peer_sub_keys, peer_down, peer_up, final_norm_w):
    batch, seq, d = x.shape
    depth = w_in.shape[0]
    assert depth == 1, "the final RMSNorm is fused into the (single) layer's last stage"
    out = _layer(x.reshape(batch * seq, d), batch, seq, norm_mix_w[0], w_in[0], gate_bias[0],
                 conv_dw_w[0], conv_dw_b[0], conv_ln_w[0], conv_ln_b[0], w_conv_out[0],
                 dn_conv_w[0], dn_a_log[0], dn_dt_bias[0], dn_norm_w[0], w_dn_out[0], w_out[0],
                 norm_ffn_w[0], peer_w_query[0], peer_sub_keys[0], peer_down[0], peer_up[0],
                 final_norm_w)
    return out.reshape(batch, seq, d)
```
